```python
import math
import jax, jax.numpy as jnp
from jax import lax
import numpy as np

D_MODEL = 1024
BATCH = 16
SEQ = 2048
DEPTH = 1
DEC_BATCH = 128
DEC_SEQ = 8
PAST_LEN = 8192
PAGE_SIZE = 128

N_HEADS = 8
HEAD_DIM = 64
V_DIM = 2 * HEAD_DIM
D_ATTN = N_HEADS * V_DIM
D_CONV = D_MODEL
CONV_WIDTH = 31
Q_BLOCK = 128
RMS_EPS = 1e-6
SUBLN_EPS = 1e-5
LN_EPS = 1e-5
NEG_INIT = -1e30
SPLIT_SIZES = (D_CONV, D_CONV, D_CONV, N_HEADS * 2 * HEAD_DIM, N_HEADS * 2 * HEAD_DIM, D_ATTN, D_ATTN, D_MODEL, D_MODEL)
D_IN = sum(SPLIT_SIZES)
SPLIT_POINTS = tuple(int(i) for i in np.cumsum(SPLIT_SIZES)[:-1])

kernel_name = "gated_conformer_diffattn_step"


def _rmsnorm(x, g, eps=RMS_EPS):
    xf = x.astype(jnp.float32)
    y = xf * lax.rsqrt(jnp.mean(xf * xf, axis=-1, keepdims=True) + eps)
    return (y * g.astype(jnp.float32)).astype(x.dtype)


def _layernorm(x, g, b, eps=LN_EPS):
    xf = x.astype(jnp.float32)
    mu = jnp.mean(xf, axis=-1, keepdims=True)
    var = jnp.mean(jnp.square(xf - mu), axis=-1, keepdims=True)
    y = (xf - mu) * lax.rsqrt(var + eps) * g.astype(jnp.float32) + b.astype(jnp.float32)
    return y.astype(x.dtype)


def _lambda_init(layer):
    return 0.8 - 0.6 * math.exp(-0.3 * layer)


def _causal_dwconv(u, buf, w, b):
    full = jnp.concatenate([buf.astype(u.dtype), u], axis=1)
    y = lax.conv_general_dilated(full, w[:, None, :].astype(u.dtype), window_strides=(1,), padding="VALID",
                                 dimension_numbers=("NWC", "WIO", "NWC"), feature_group_count=u.shape[-1])
    return y + b.astype(u.dtype), full[:, full.shape[1] - (CONV_WIDTH - 1):]


def _scores(q, k):
    return jnp.einsum("bqhmd,bkhmd->bhmqk", q, k, preferred_element_type=jnp.float32) * (HEAD_DIM ** -0.5)


def _diff_attn_prompt(q, k, v, lam):
    b, s = q.shape[:2]
    nb = s // Q_BLOCK
    qb = q.reshape(b, nb, Q_BLOCK, N_HEADS, 2, HEAD_DIM).transpose(1, 0, 2, 3, 4, 5)
    kpos = jnp.arange(s)
    vf = v.astype(jnp.float32)

    def block(args):
        qi, i = args
        sc = _scores(qi, k)
        qpos = i * Q_BLOCK + jnp.arange(Q_BLOCK)
        mask = kpos[None, :] <= qpos[:, None]
        p = jax.nn.softmax(jnp.where(mask, sc, -jnp.inf), axis=-1)
        o = jnp.einsum("bhmqk,bkhe->bhmqe", p, vf)
        return o[:, :, 0] - lam * o[:, :, 1]

    o = lax.map(block, (qb, jnp.arange(nb)))
    return o.transpose(1, 0, 3, 2, 4).reshape(b, s, N_HEADS, V_DIM)


def _online_update(carry, s, v):
    m, l, acc = carry
    m_new = jnp.maximum(m, jnp.max(s, axis=-1))
    alpha = jnp.exp(m - m_new)
    p = jnp.exp(s - m_new[..., None])
    l_new = l * alpha + jnp.sum(p, axis=-1)
    acc_new = acc * alpha[..., None] + jnp.einsum("bhmqk,bkhe->bhmqe", p, v.astype(jnp.float32))
    return m_new, l_new, acc_new


def _diff_attn_sample(q, k, v, lam, cache_k, cache_v, page_table, layer):
    b, t = q.shape[:2]
    shp = (b, N_HEADS, 2, t)
    carry = (jnp.full(shp, NEG_INIT, jnp.float32), jnp.zeros(shp, jnp.float32),
             jnp.zeros(shp + (V_DIM,), jnp.float32))

    def page_step(c, phys):
        kp = cache_k[layer, phys].reshape(b, PAGE_SIZE, N_HEADS, 2, HEAD_DIM)
        vp = cache_v[layer, phys]
        return _online_update(c, _scores(q, kp.astype(q.dtype)), vp), None

    carry, _ = lax.scan(page_step, carry, page_table.T)
    causal = jnp.tril(jnp.ones((t, t), dtype=bool))
    s = jnp.where(causal, _scores(q, k), -jnp.inf)
    m, l, acc = _online_update(carry, s, v)
    o = acc / l[..., None]
    o = o[:, :, 0] - lam * o[:, :, 1]
    return o.transpose(0, 2, 1, 3)


def _hybrid_layer(x, conv_buf, attend, layer, norm_g, w_in, conv_w, conv_b, conv_ln_g, conv_ln_b, w_conv_proj,
                  lambda_q1, lambda_k1, lambda_q2, lambda_k2, subln_g, w_attn_proj, w_out):
    b, t, _ = x.shape
    h = _rmsnorm(x, norm_g)
    z = h @ w_in
    glu_a, glu_b, cgate, q, k, v, agate, g_conv, g_attn = jnp.split(z, SPLIT_POINTS, axis=-1)
    u = glu_a * jax.nn.sigmoid(glu_b)
    c, new_buf = _causal_dwconv(u, conv_buf, conv_w, conv_b)
    c = jax.nn.silu(_layernorm(c, conv_ln_g, conv_ln_b)) * jax.nn.silu(cgate)
    y_conv = c @ w_conv_proj
    q = q.reshape(b, t, N_HEADS, 2, HEAD_DIM)
    k = k.reshape(b, t, N_HEADS, 2, HEAD_DIM)
    v = v.reshape(b, t, N_HEADS, V_DIM)
    lam_init = _lambda_init(layer)
    f32 = jnp.float32
    lam = (jnp.exp(jnp.sum(lambda_q1.astype(f32) * lambda_k1.astype(f32)))
           - jnp.exp(jnp.sum(lambda_q2.astype(f32) * lambda_k2.astype(f32))) + lam_init)
    o = attend(q, k, v, lam)
    o = _rmsnorm(o, subln_g, SUBLN_EPS) * (1.0 - lam_init)
    o = o.reshape(b, t, D_ATTN).astype(x.dtype) * jax.nn.silu(agate)
    y_attn = o @ w_attn_proj
    merged = jax.nn.sigmoid(g_conv) * y_conv + jax.nn.sigmoid(g_attn) * y_attn
    y = x + merged @ w_out
    return y, new_buf, k.reshape(b, t, N_HEADS, 2 * HEAD_DIM), v


def setup_inputs(seed: int = 0) -> dict:
    key = jax.random.key(seed)
    ks = jax.random.split(key, 24)
    f32 = jnp.float32
    n_pages = PAST_LEN // PAGE_SIZE
    n_used = DEC_BATCH * n_pages
    n_pool = n_used + (n_used + 3) // 4

    def nrm(k, shape, s=1.0):
        return s * jax.random.normal(k, shape, f32)

    page_table = jax.random.permutation(ks[0], n_pool)[:n_used].reshape(DEC_BATCH, n_pages).astype(jnp.int32)
    return {
        "x_prompt": nrm(ks[1], (BATCH, SEQ, D_MODEL)),
        "x_sample": nrm(ks[2], (DEC_BATCH, DEC_SEQ, D_MODEL)),
        "cache_k": nrm(ks[3], (DEPTH, n_pool, PAGE_SIZE, N_HEADS, 2 * HEAD_DIM)),
        "cache_v": nrm(ks[4], (DEPTH, n_pool, PAGE_SIZE, N_HEADS, V_DIM)),
        "state_conv": nrm(ks[5], (DEPTH, DEC_BATCH, CONV_WIDTH - 1, D_CONV), 0.5),
        "page_table": page_table,
        "norm_g": 1.0 + nrm(ks[6], (DEPTH, D_MODEL), 0.02),
        "w_in": nrm(ks[7], (DEPTH, D_MODEL, D_IN), D_MODEL ** -0.5),
        "conv_w": nrm(ks[8], (DEPTH, CONV_WIDTH, D_CONV), CONV_WIDTH ** -0.5),
        "conv_b": nrm(ks[9], (DEPTH, D_CONV), 0.02),
        "conv_ln_g": 1.0 + nrm(ks[10], (DEPTH, D_CONV), 0.02),
        "conv_ln_b": nrm(ks[11], (DEPTH, D_CONV), 0.02),
        "w_conv_proj": nrm(ks[12], (DEPTH, D_CONV, D_MODEL), D_CONV ** -0.5),
        "lambda_q1": nrm(ks[13], (DEPTH, HEAD_DIM), 0.1),
        "lambda_k1": nrm(ks[14], (DEPTH, HEAD_DIM), 0.1),
        "lambda_q2": nrm(ks[15], (DEPTH, HEAD_DIM), 0.1),
        "lambda_k2": nrm(ks[16], (DEPTH, HEAD_DIM), 0.1),
        "subln_g": 1.0 + nrm(ks[17], (DEPTH, V_DIM), 0.02),
        "w_attn_proj": nrm(ks[18], (DEPTH, D_ATTN, D_MODEL), D_ATTN ** -0.5),
        "w_out": nrm(ks[19], (DEPTH, D_MODEL, D_MODEL), D_MODEL ** -0.5),
        "final_norm_g": 1.0 + nrm(ks[20], (D_MODEL,), 0.02),
    }


def reference(x_prompt, x_sample, cache_k, cache_v, state_conv, page_table, norm_g, w_in, conv_w, conv_b,
              conv_ln_g, conv_ln_b, w_conv_proj, lambda_q1, lambda_k1, lambda_q2, lambda_k2, subln_g,
              w_attn_proj, w_out, final_norm_g):
    xp, xs = x_prompt, x_sample
    kp_l, vp_l, bp_l, ks_l, vs_l, bs_l = [], [], [], [], [], []
    for layer in range(DEPTH):
        params = (norm_g[layer], w_in[layer], conv_w[layer], conv_b[layer], conv_ln_g[layer], conv_ln_b[layer],
                  w_conv_proj[layer], lambda_q1[layer], lambda_k1[layer], lambda_q2[layer], lambda_k2[layer],
                  subln_g[layer], w_attn_proj[layer], w_out[layer])

        def attend_sample(q, k, v, lam, _layer=layer):
            return _diff_attn_sample(q, k, v, lam, cache_k, cache_v, page_table, _layer)

        zero_buf = jnp.zeros((xp.shape[0], CONV_WIDTH - 1, D_CONV), xp.dtype)
        xp, bp, kp, vp = _hybrid_layer(xp, zero_buf, _diff_attn_prompt, layer, *params)
        xs, bs, ks, vs = _hybrid_layer(xs, state_conv[layer], attend_sample, layer, *params)
        kp_l.append(kp); vp_l.append(vp); bp_l.append(bp)
        ks_l.append(ks); vs_l.append(vs); bs_l.append(bs)
    y_prompt = _rmsnorm(xp, final_norm_g)
    y_sample = _rmsnorm(xs, final_norm_g)
    return (y_prompt, y_sample, jnp.stack(kp_l), jnp.stack(vp_l), jnp.stack(bp_l),
            jnp.stack(ks_l), jnp.stack(vs_l), jnp.stack(bs_l))
```

```python
import functools
import math

import jax
import jax.numpy as jnp
from jax import lax
from jax.experimental import pallas as pl
from jax.experimental.pallas import tpu as pltpu

F32 = jnp.float32
BF16 = jnp.bfloat16

N_HEADS = 8
HEAD_DIM = 64
V_DIM = 2 * HEAD_DIM
CONV_WIDTH = 31
CONV_HALO = 32
HALO_PAD = CONV_HALO - (CONV_WIDTH - 1)
RMS_EPS = 1e-6
SUBLN_EPS = 1e-5
LN_EPS = 1e-5
NEG_INIT = -1e30
LAMBDA_INIT = 0.8 - 0.6 * math.exp(-0.3 * 0)
N_SPLITS = 9
V7X_VMEM_LIMIT = 56 * 1024 * 1024

_NT = (((1,), (1,)), ((), ()))
_TN = (((0,), (0,)), ((), ()))


def _sigmoid(z):
    return 1.0 / (1.0 + jnp.exp(-z))


def _silu(z):
    return z * _sigmoid(z)


def _lambda_value(lq1, lk1, lq2, lk2):
    a = jnp.sum(lq1 * lk1, axis=-1, keepdims=True)
    b = jnp.sum(lq2 * lk2, axis=-1, keepdims=True)
    return jnp.exp(a) - jnp.exp(b) + LAMBDA_INIT


def _inproj_body(x_ref, g_ref, w_ref, u_ref, cg_ref, q_ref, k32_ref, kb_ref, v32_ref, vb_ref,
                 ag_ref, sc_ref, sa_ref):
    d = x_ref.shape[-1]
    x = x_ref[...]
    ms = jnp.mean(x * x, axis=-1, keepdims=True)
    h = (x * lax.rsqrt(ms + RMS_EPS) * g_ref[...]).astype(BF16)

    def proj(i):
        return jnp.dot(h, w_ref[:, i * d:(i + 1) * d], preferred_element_type=F32)

    u_ref[...] = (proj(0) * _sigmoid(proj(1))).astype(BF16)
    cg_ref[...] = _silu(proj(2)).astype(BF16)
    q_ref[...] = (proj(3) * (HEAD_DIM ** -0.5)).astype(BF16)
    k = proj(4)
    k32_ref[...] = k
    kb_ref[...] = k.astype(BF16)
    v = proj(5)
    v32_ref[...] = v
    vb_ref[...] = v.astype(BF16)
    ag_ref[...] = _silu(proj(6)).astype(BF16)
    sc_ref[...] = _sigmoid(proj(7)).astype(BF16)
    sa_ref[...] = _sigmoid(proj(8)).astype(BF16)


def _in_projection(x2, norm_g, w_in_bf, tm):
    m, d = x2.shape
    d_in = w_in_bf.shape[1]
    row = pl.BlockSpec((tm, d), lambda i: (i, 0))
    out_dtypes = [BF16, BF16, BF16, F32, BF16, F32, BF16, BF16, BF16, BF16]
    return pl.pallas_call(
        _inproj_body,
        grid=(m // tm,),
        in_specs=[row,
                  pl.BlockSpec((1, d), lambda i: (0, 0)),
                  pl.BlockSpec((d, d_in), lambda i: (0, 0), pipeline_mode=pl.Buffered(1))],
        out_specs=[row] * len(out_dtypes),
        out_shape=[jax.ShapeDtypeStruct((m, d), t) for t in out_dtypes],
        compiler_params=pltpu.CompilerParams(dimension_semantics=("parallel",),
                                             vmem_limit_bytes=V7X_VMEM_LIMIT),
        name="in_projection",
    )(x2, norm_g.reshape(1, d), w_in_bf)


def _conv_body(u_ref, cg_ref, sc_ref, buf_ref, cw_ref, cb_ref, lg_ref, lb_ref, w_ref,
               gc_ref, nbuf_ref, full_ref):
    bb, ts, d = u_ref.shape
    s = pl.program_id(1)

    @pl.when(s == 0)
    def _():
        full_ref[:, 0:CONV_HALO, :] = buf_ref[...]

    full_ref[:, CONV_HALO:CONV_HALO + ts, :] = u_ref[...].astype(F32)
    acc = jnp.broadcast_to(cb_ref[...].reshape(1, 1, d), (bb, ts, d))
    for j in range(CONV_WIDTH):
        tap = cw_ref[j:j + 1, :].reshape(1, 1, d)
        acc = acc + tap * full_ref[:, HALO_PAD + j:HALO_PAD + j + ts, :]
    mu = jnp.mean(acc, axis=-1, keepdims=True)
    cen = acc - mu
    var = jnp.mean(cen * cen, axis=-1, keepdims=True)
    y = cen * lax.rsqrt(var + LN_EPS) * lg_ref[...].reshape(1, 1, d) + lb_ref[...].reshape(1, 1, d)
    c = _silu(y) * cg_ref[...].astype(F32)
    yc = jnp.dot(c.astype(BF16).reshape(bb * ts, d), w_ref[...], preferred_element_type=F32)
    gc_ref[...] = (sc_ref[...].astype(F32) * yc.reshape(bb, ts, d)).astype(BF16)
    tail = full_ref[:, ts:ts + CONV_HALO, :]
    full_ref[:, 0:CONV_HALO, :] = tail
    nbuf_ref[...] = tail


def _conv_branch(u3, cg3, sc3, buf32, conv_w, conv_b, ln_g, ln_b, w_proj_bf, bb, ts):
    b, s, d = u3.shape
    tile = pl.BlockSpec((bb, ts, d), lambda i, j: (i, j, 0))
    halo = pl.BlockSpec((bb, CONV_HALO, d), lambda i, j: (i, 0, 0))
    vec = pl.BlockSpec((1, d), lambda i, j: (0, 0))
    return pl.pallas_call(
        _conv_body,
        grid=(b // bb, s // ts),
        in_specs=[tile, tile, tile, halo,
                  pl.BlockSpec((CONV_WIDTH, d), lambda i, j: (0, 0)),
                  vec, vec, vec,
                  pl.BlockSpec((d, d), lambda i, j: (0, 0))],
        out_specs=[tile, halo],
        out_shape=[jax.ShapeDtypeStruct((b, s, d), BF16),
                   jax.ShapeDtypeStruct((b, CONV_HALO, d), F32)],
        scratch_shapes=[pltpu.VMEM((bb, CONV_HALO + ts, d), F32)],
        compiler_params=pltpu.CompilerParams(dimension_semantics=("parallel", "arbitrary"),
                                             vmem_limit_bytes=V7X_VMEM_LIMIT),
        name="conv_branch",
    )(u3, cg3, sc3, buf32, conv_w, conv_b.reshape(1, d), ln_g.reshape(1, d), ln_b.reshape(1, d), w_proj_bf)


def _head_post(o1, o2, lam, sg, ag):
    dlt = o1 - lam * o2
    ms = jnp.mean(dlt * dlt, axis=-1, keepdims=True)
    r = dlt * lax.rsqrt(ms + SUBLN_EPS) * sg * (1.0 - LAMBDA_INIT)
    return (r * ag.astype(F32)).astype(BF16)


def _prompt_attn_body(lq1_ref, lk1_ref, lq2_ref, lk2_ref, sg_ref, q_ref, k_ref, v_ref, ag_ref,
                      o_ref, m_sc, l_sc, acc_sc):
    tq, e = q_ref.shape
    qi = pl.program_id(2)
    q = q_ref[...]
    lane = lax.broadcasted_iota(jnp.int32, (tq, e), 1)
    zero = jnp.zeros_like(q)
    qs = jnp.concatenate([jnp.where(lane < HEAD_DIM, q, zero), jnp.where(lane >= HEAD_DIM, q, zero)], axis=0)
    m_sc[...] = jnp.full(m_sc.shape, NEG_INIT, F32)
    l_sc[...] = jnp.zeros(l_sc.shape, F32)
    acc_sc[...] = jnp.zeros(acc_sc.shape, F32)

    def step(j, masked):
        start = pl.multiple_of(j * tq, tq)
        kb = k_ref[pl.ds(start, tq), :]
        vb = v_ref[pl.ds(start, tq), :]
        s = lax.dot_general(qs, kb, _NT, preferred_element_type=F32)
        if masked:
            row = lax.broadcasted_iota(jnp.int32, s.shape, 0)
            col = lax.broadcasted_iota(jnp.int32, s.shape, 1)
            qpos = jnp.where(row >= tq, row - tq, row)
            s = jnp.where(col <= qpos, s, NEG_INIT)
        m_prev = m_sc[...]
        m_new = jnp.maximum(m_prev, jnp.max(s, axis=-1, keepdims=True))
        alpha = jnp.exp(m_prev - m_new)
        p = jnp.exp(s - m_new)
        l_sc[...] = l_sc[...] * alpha + jnp.sum(p, axis=-1, keepdims=True)
        acc_sc[...] = acc_sc[...] * alpha + jnp.dot(p.astype(BF16), vb, preferred_element_type=F32)
        m_sc[...] = m_new

    def loop_body(j, carry):
        step(j, False)
        return carry

    lax.fori_loop(0, qi, loop_body, 0)
    step(qi, True)

    o = acc_sc[...] / l_sc[...]
    lam = _lambda_value(lq1_ref[...], lk1_ref[...], lq2_ref[...], lk2_ref[...])
    o_ref[...] = _head_post(o[:tq], o[tq:], lam, sg_ref[...], ag_ref[...])


def _prompt_attention(q3, k3, v3, ag3, lq1, lk1, lq2, lk2, subln_g, tq):
    b, s, d = q3.shape
    lam_spec = pl.BlockSpec((1, HEAD_DIM), lambda i, h, j: (0, 0))
    qtile = pl.BlockSpec((None, tq, V_DIM), lambda i, h, j: (i, j, h))
    kv = pl.BlockSpec((None, s, V_DIM), lambda i, h, j: (i, 0, h))
    return pl.pallas_call(
        _prompt_attn_body,
        grid=(b, N_HEADS, s // tq),
        in_specs=[lam_spec, lam_spec, lam_spec, lam_spec,
                  pl.BlockSpec((1, V_DIM), lambda i, h, j: (0, 0)),
                  qtile, kv, kv, qtile],
        out_specs=qtile,
        out_shape=jax.ShapeDtypeStruct((b, s, d), BF16),
        scratch_shapes=[pltpu.VMEM((2 * tq, 1), F32), pltpu.VMEM((2 * tq, 1), F32),
                        pltpu.VMEM((2 * tq, V_DIM), F32)],
        compiler_params=pltpu.CompilerParams(dimension_semantics=("parallel", "parallel", "parallel"),
                                             vmem_limit_bytes=V7X_VMEM_LIMIT),
        name="prompt_attention",
    )(lq1.reshape(1, -1), lk1.reshape(1, -1), lq2.reshape(1, -1), lk2.reshape(1, -1),
      subln_g.reshape(1, -1), q3, k3, v3, ag3)


def _paged_attn_body(n_pages_per_step, pt_ref, wt_ref, knew_ref, vnew_ref, *rest):
    del pt_ref
    pps = n_pages_per_step
    k_refs = rest[:pps]
    v_refs = rest[pps:2 * pps]
    o_ref, m_sc, l_sc, acc_sc, w_sc = rest[2 * pps:]
    page, nh, e = k_refs[0].shape
    ncol = wt_ref.shape[0]
    c = pl.program_id(1)

    sub = lax.broadcasted_iota(jnp.int32, (nh, ncol), 0)
    lane = lax.broadcasted_iota(jnp.int32, (nh, ncol), 1)
    own_head = (lane // (ncol // nh)) == sub

    @pl.when(c == 0)
    def _():
        m_sc[...] = jnp.full(m_sc.shape, NEG_INIT, F32)
        l_sc[...] = jnp.zeros(l_sc.shape, F32)
        acc_sc[...] = jnp.zeros(acc_sc.shape, F32)
        r = lax.broadcasted_iota(jnp.int32, (ncol, e), 0)
        f = lax.broadcasted_iota(jnp.int32, (ncol, e), 1)
        t_new = ncol // (2 * nh)
        same_map = ((r // t_new) % 2) == (f // HEAD_DIM)
        w_sc[...] = jnp.where(same_map, wt_ref[...], jnp.zeros_like(wt_ref[...]))

    wt = w_sc[...]

    def update(scores, values):
        m_prev = m_sc[...]
        m_new = m_prev
        for r in scores:
            m_new = jnp.maximum(m_new, jnp.max(r, axis=0))
        alpha = jnp.exp(m_prev - m_new)
        l_new = l_sc[...] * alpha
        pv = jnp.zeros(acc_sc.shape, F32)
        for r, vf in zip(scores, values):
            p = jnp.exp(r - m_new[None])
            l_new = l_new + jnp.sum(p, axis=0)
            pm = jnp.where(own_head[None], p, 0.0).astype(BF16).reshape(vf.shape[0], ncol)
            pv = pv + lax.dot_general(vf, pm, _TN, preferred_element_type=F32)
        alpha_row = jnp.sum(jnp.where(own_head, alpha, 0.0), axis=0, keepdims=True)
        acc_sc[...] = acc_sc[...] * alpha_row + pv
        l_sc[...] = l_new
        m_sc[...] = m_new

    scores, values = [], []
    for i in range(pps):
        kf = k_refs[i][...].reshape(page * nh, e).astype(BF16)
        r = lax.dot_general(kf, wt, _NT, preferred_element_type=F32)
        scores.append(r.reshape(page, nh, ncol))
        values.append(v_refs[i][...].reshape(page * nh, e).astype(BF16))
    update(scores, values)

    @pl.when(c == pl.num_programs(1) - 1)
    def _():
        t_new = knew_ref.shape[0]
        kf = knew_ref[...].reshape(t_new * nh, e).astype(BF16)
        r = lax.dot_general(kf, wt, _NT, preferred_element_type=F32).reshape(t_new, nh, ncol)
        key_pos = lax.broadcasted_iota(jnp.int32, r.shape, 0)
        q_pos = lax.broadcasted_iota(jnp.int32, r.shape, 2) % t_new
        r = jnp.where(key_pos <= q_pos, r, NEG_INIT)
        update([r], [vnew_ref[...].reshape(t_new * nh, e).astype(BF16)])
        l_row = jnp.sum(jnp.where(own_head, l_sc[...], 0.0), axis=0, keepdims=True)
        o_ref[...] = acc_sc[...] / l_row


def _paged_attention(page_table, wt, cache_k, cache_v, k_new, v_new, pps):
    nb, n_pages = page_table.shape
    _, _, page, nh, e = cache_k.shape
    ncol = wt.shape[1]
    t_new = k_new.shape[1]

    def page_spec(i):
        return pl.BlockSpec((None, None, page, nh, e),
                            lambda b, c, pt: (0, pt[b, c * pps + i], 0, 0, 0))

    new_spec = pl.BlockSpec((None, t_new, nh, e), lambda b, c, pt: (b, 0, 0, 0))
    grid_spec = pltpu.PrefetchScalarGridSpec(
        num_scalar_prefetch=1,
        grid=(nb, n_pages // pps),
        in_specs=[pl.BlockSpec((None, ncol, e), lambda b, c, pt: (b, 0, 0)), new_spec, new_spec]
                 + [page_spec(i) for i in range(pps)] * 2,
        out_specs=pl.BlockSpec((None, e, ncol), lambda b, c, pt: (b, 0, 0)),
        scratch_shapes=[pltpu.VMEM((nh, ncol), F32), pltpu.VMEM((nh, ncol), F32),
                        pltpu.VMEM((e, ncol), F32), pltpu.VMEM((ncol, e), BF16)],
    )
    return pl.pallas_call(
        functools.partial(_paged_attn_body, pps),
        grid_spec=grid_spec,
        out_shape=jax.ShapeDtypeStruct((nb, e, ncol), F32),
        compiler_params=pltpu.CompilerParams(dimension_semantics=("parallel", "arbitrary"),
                                             vmem_limit_bytes=V7X_VMEM_LIMIT),
        name="paged_attention",
    )(page_table, wt, k_new, v_new, *([cache_k] * pps), *([cache_v] * pps))


def _sample_post_body(lq1_ref, lk1_ref, lq2_ref, lk2_ref, sg_ref, o1_ref, o2_ref, ag_ref, og_ref):
    lam = _lambda_value(lq1_ref[...], lk1_ref[...], lq2_ref[...], lk2_ref[...])
    for h in range(N_HEADS):
        cols = slice(h * V_DIM, (h + 1) * V_DIM)
        og_ref[:, cols] = _head_post(o1_ref[:, cols], o2_ref[:, cols], lam, sg_ref[...], ag_ref[:, cols])


def _sample_post(o1, o2, ag, lq1, lk1, lq2, lk2, subln_g):
    m, d = o1.shape
    full = pl.BlockSpec((m, d), lambda i: (0, 0))
    lam_spec = pl.BlockSpec((1, HEAD_DIM), lambda i: (0, 0))
    return pl.pallas_call(
        _sample_post_body,
        grid=(1,),
        in_specs=[lam_spec, lam_spec, lam_spec, lam_spec, pl.BlockSpec((1, V_DIM), lambda i: (0, 0)),
                  full, full, full],
        out_specs=full,
        out_shape=jax.ShapeDtypeStruct((m, d), BF16),
        compiler_params=pltpu.CompilerParams(vmem_limit_bytes=V7X_VMEM_LIMIT),
        name="sample_attention_post",
    )(lq1.reshape(1, -1), lk1.reshape(1, -1), lq2.reshape(1, -1), lk2.reshape(1, -1),
      subln_g.reshape(1, -1), o1, o2, ag)


def _out_body(og_ref, gc_ref, sa_ref, x_ref, wap_ref, wo_ref, gf_ref, y_ref):
    ya = jnp.dot(og_ref[...], wap_ref[...], preferred_element_type=F32)
    merged = gc_ref[...].astype(F32) + sa_ref[...].astype(F32) * ya
    y = x_ref[...] + jnp.dot(merged.astype(BF16), wo_ref[...], preferred_element_type=F32)
    ms = jnp.mean(y * y, axis=-1, keepdims=True)
    y_ref[...] = y * lax.rsqrt(ms + RMS_EPS) * gf_ref[...]


def _out_projection(og, gc, sa, x2, w_attn_bf, w_out_bf, final_g, tm):
    m, d = x2.shape
    row = pl.BlockSpec((tm, d), lambda i: (i, 0))
    wspec = pl.BlockSpec((d, d), lambda i: (0, 0))
    return pl.pallas_call(
        _out_body,
        grid=(m // tm,),
        in_specs=[row, row, row, row, wspec, wspec, pl.BlockSpec((1, d), lambda i: (0, 0))],
        out_specs=row,
        out_shape=jax.ShapeDtypeStruct((m, d), F32),
        compiler_params=pltpu.CompilerParams(dimension_semantics=("parallel",),
                                             vmem_limit_bytes=V7X_VMEM_LIMIT),
        name="out_projection",
    )(og, gc, sa, x2, w_attn_bf, w_out_bf, final_g.reshape(1, d))


def kernel(x_prompt, x_sample, cache_k, cache_v, state_conv, page_table, norm_g, w_in, conv_w, conv_b,
           conv_ln_g, conv_ln_b, w_conv_proj, lambda_q1, lambda_k1, lambda_q2, lambda_k2, subln_g,
           w_attn_proj, w_out, final_norm_g):
    depth = norm_g.shape[0]
    assert depth == 1, "single-layer step"
    b, s, d = x_prompt.shape
    nb, t_new, _ = x_sample.shape
    w_in_bf = w_in[0].astype(BF16)
    w_conv_bf = w_conv_proj[0].astype(BF16)
    w_attn_bf = w_attn_proj[0].astype(BF16)
    w_out_bf = w_out[0].astype(BF16)
    lams = (lambda_q1[0], lambda_k1[0], lambda_q2[0], lambda_k2[0])

    def conv_and_state(u, cg, sc, buf, bsz, seq, bb, ts):
        buf32 = jnp.pad(buf, ((0, 0), (HALO_PAD, 0), (0, 0)))
        gc, nbuf = _conv_branch(u.reshape(bsz, seq, d), cg.reshape(bsz, seq, d), sc.reshape(bsz, seq, d), buf32,
                                conv_w[0], conv_b[0], conv_ln_g[0], conv_ln_b[0], w_conv_bf, bb, ts)
        return gc.reshape(bsz * seq, d), nbuf[:, HALO_PAD:, :]

    xp2 = x_prompt.reshape(b * s, d)
    u, cg, q, k32, kb, v32, vb, ag, sc, sa = _in_projection(xp2, norm_g[0], w_in_bf, 256)
    gc, conv_prompt = conv_and_state(u, cg, sc, jnp.zeros((b, CONV_WIDTH - 1, d), F32), b, s, 1, 256)
    og = _prompt_attention(q.reshape(b, s, d), kb.reshape(b, s, d), vb.reshape(b, s, d), ag.reshape(b, s, d),
                           *lams, subln_g[0], 256)
    y_prompt = _out_projection(og.reshape(b * s, d), gc, sa, xp2, w_attn_bf, w_out_bf, final_norm_g, 512)
    k_prompt = k32.reshape(1, b, s, N_HEADS, V_DIM)
    v_prompt = v32.reshape(1, b, s, N_HEADS, V_DIM)

    xs2 = x_sample.reshape(nb * t_new, d)
    u, cg, q, k32, kb, v32, vb, ag, sc, sa = _in_projection(xs2, norm_g[0], w_in_bf, 256)
    gc, conv_sample = conv_and_state(u, cg, sc, state_conv[0], nb, t_new, 8, t_new)
    k_sample = k32.reshape(nb, t_new, N_HEADS, V_DIM)
    v_sample = v32.reshape(nb, t_new, N_HEADS, V_DIM)
    qh = q.reshape(nb, t_new, N_HEADS, V_DIM).transpose(0, 2, 1, 3)
    wt = jnp.broadcast_to(qh[:, :, None], (nb, N_HEADS, 2, t_new, V_DIM)).reshape(nb, N_HEADS * 2 * t_new, V_DIM)
    ot = _paged_attention(page_table, wt, cache_k, cache_v, k_sample, v_sample, 8)
    o5 = ot.reshape(nb, V_DIM, N_HEADS, 2, t_new).transpose(3, 0, 4, 2, 1)
    o1 = o5[0].reshape(nb * t_new, d)
    o2 = o5[1].reshape(nb * t_new, d)
    og = _sample_post(o1, o2, ag, *lams, subln_g[0])
    y_sample = _out_projection(og, gc, sa, xs2, w_attn_bf, w_out_bf, final_norm_g, 256)

    return (y_prompt.reshape(b, s, d), y_sample.reshape(nb, t_new, d), k_prompt, v_prompt, conv_prompt[None],
            k_sample[None], v_sample[None], conv_sample[None])
```

```python
import functools
import math

import jax
import jax.numpy as jnp
from jax import lax
from jax.experimental import pallas as pl
from jax.experimental.pallas import tpu as pltpu

F32 = jnp.float32
BF16 = jnp.bfloat16

N_HEADS = 8
HEAD_DIM = 64
V_DIM = 2 * HEAD_DIM
CONV_WIDTH = 31
CONV_HALO = 32
HALO_PAD = CONV_HALO - (CONV_WIDTH - 1)
RMS_EPS = 1e-6
SUBLN_EPS = 1e-5
LN_EPS = 1e-5
NEG_INIT = -1e30
LAMBDA_INIT = 0.8 - 0.6 * math.exp(-0.3 * 0)
Q_SCALE = HEAD_DIM ** -0.5 * math.log2(math.e)
N_SPLITS = 9
V7X_VMEM_LIMIT = 56 * 1024 * 1024

_NT = (((1,), (1,)), ((), ()))
_TN = (((0,), (0,)), ((), ()))


def _sigmoid(z):
    return 1.0 / (1.0 + jnp.exp(-z))


def _silu(z):
    return z * _sigmoid(z)


def _lambda_value(lq1, lk1, lq2, lk2):
    a = jnp.sum(lq1 * lk1, axis=-1, keepdims=True)
    b = jnp.sum(lq2 * lk2, axis=-1, keepdims=True)
    return jnp.exp(a) - jnp.exp(b) + LAMBDA_INIT


def _inproj_body(x_ref, g_ref, w_ref, u_ref, cg_ref, q_ref, k32_ref, kb_ref, v32_ref, vb_ref,
                 ag_ref, sc_ref, sa_ref):
    d = x_ref.shape[-1]
    x = x_ref[...]
    ms = jnp.mean(x * x, axis=-1, keepdims=True)
    h = (x * lax.rsqrt(ms + RMS_EPS) * g_ref[...]).astype(BF16)

    def proj(i):
        return jnp.dot(h, w_ref[:, i * d:(i + 1) * d], preferred_element_type=F32)

    u_ref[...] = (proj(0) * _sigmoid(proj(1))).astype(BF16)
    cg_ref[...] = _silu(proj(2)).astype(BF16)
    q_ref[...] = (proj(3) * Q_SCALE).astype(BF16)
    k = proj(4)
    k32_ref[...] = k
    kb_ref[...] = k.astype(BF16)
    v = proj(5)
    v32_ref[...] = v
    vb_ref[...] = v.astype(BF16)
    ag_ref[...] = _silu(proj(6)).astype(BF16)
    sc_ref[...] = _sigmoid(proj(7)).astype(BF16)
    sa_ref[...] = _sigmoid(proj(8)).astype(BF16)


def _in_projection(x2, norm_g, w_in_bf, tm):
    m, d = x2.shape
    d_in = w_in_bf.shape[1]
    row = pl.BlockSpec((tm, d), lambda i: (i, 0))
    out_dtypes = [BF16, BF16, BF16, F32, BF16, F32, BF16, BF16, BF16, BF16]
    return pl.pallas_call(
        _inproj_body,
        grid=(m // tm,),
        in_specs=[row,
                  pl.BlockSpec((1, d), lambda i: (0, 0)),
                  pl.BlockSpec((d, d_in), lambda i: (0, 0), pipeline_mode=pl.Buffered(1))],
        out_specs=[row] * len(out_dtypes),
        out_shape=[jax.ShapeDtypeStruct((m, d), t) for t in out_dtypes],
        compiler_params=pltpu.CompilerParams(dimension_semantics=("parallel",),
                                             vmem_limit_bytes=V7X_VMEM_LIMIT),
        name="in_projection",
    )(x2, norm_g.reshape(1, d), w_in_bf)


def _conv_body(u_ref, cg_ref, sc_ref, buf_ref, cw_ref, cb_ref, lg_ref, lb_ref, w_ref,
               gc_ref, nbuf_ref, full_ref):
    bb, ts, d = u_ref.shape
    s = pl.program_id(1)

    @pl.when(s == 0)
    def _():
        full_ref[:, 0:CONV_HALO, :] = buf_ref[...]

    full_ref[:, CONV_HALO:CONV_HALO + ts, :] = u_ref[...].astype(F32)
    acc = jnp.broadcast_to(cb_ref[...].reshape(1, 1, d), (bb, ts, d))
    for j in range(CONV_WIDTH):
        tap = cw_ref[j:j + 1, :].reshape(1, 1, d)
        acc = acc + tap * full_ref[:, HALO_PAD + j:HALO_PAD + j + ts, :]
    mu = jnp.mean(acc, axis=-1, keepdims=True)
    cen = acc - mu
    var = jnp.mean(cen * cen, axis=-1, keepdims=True)
    y = cen * lax.rsqrt(var + LN_EPS) * lg_ref[...].reshape(1, 1, d) + lb_ref[...].reshape(1, 1, d)
    c = _silu(y) * cg_ref[...].astype(F32)
    yc = jnp.dot(c.astype(BF16).reshape(bb * ts, d), w_ref[...], preferred_element_type=F32)
    gc_ref[...] = (sc_ref[...].astype(F32) * yc.reshape(bb, ts, d)).astype(BF16)
    tail = full_ref[:, ts:ts + CONV_HALO, :]
    full_ref[:, 0:CONV_HALO, :] = tail
    nbuf_ref[...] = tail


def _conv_branch(u3, cg3, sc3, buf32, conv_w, conv_b, ln_g, ln_b, w_proj_bf, bb, ts):
    b, s, d = u3.shape
    tile = pl.BlockSpec((bb, ts, d), lambda i, j: (i, j, 0))
    halo = pl.BlockSpec((bb, CONV_HALO, d), lambda i, j: (i, 0, 0))
    vec = pl.BlockSpec((1, d), lambda i, j: (0, 0))
    return pl.pallas_call(
        _conv_body,
        grid=(b // bb, s // ts),
        in_specs=[tile, tile, tile, halo,
                  pl.BlockSpec((CONV_WIDTH, d), lambda i, j: (0, 0)),
                  vec, vec, vec,
                  pl.BlockSpec((d, d), lambda i, j: (0, 0))],
        out_specs=[tile, halo],
        out_shape=[jax.ShapeDtypeStruct((b, s, d), BF16),
                   jax.ShapeDtypeStruct((b, CONV_HALO, d), F32)],
        scratch_shapes=[pltpu.VMEM((bb, CONV_HALO + ts, d), F32)],
        compiler_params=pltpu.CompilerParams(dimension_semantics=("parallel", "arbitrary"),
                                             vmem_limit_bytes=V7X_VMEM_LIMIT),
        name="conv_branch",
    )(u3, cg3, sc3, buf32, conv_w, conv_b.reshape(1, d), ln_g.reshape(1, d), ln_b.reshape(1, d), w_proj_bf)


def _head_post(o1, o2, lam, sg, ag):
    dlt = o1 - lam * o2
    ms = jnp.mean(dlt * dlt, axis=-1, keepdims=True)
    r = dlt * lax.rsqrt(ms + SUBLN_EPS) * sg * (1.0 - LAMBDA_INIT)
    return (r * ag.astype(F32)).astype(BF16)


def _prompt_attn_body(lq1_ref, lk1_ref, lq2_ref, lk2_ref, sg_ref, q_ref, k_ref, v_ref, ag_ref,
                      o_ref, vt_sc, acc_sc):
    tq = q_ref.shape[0]
    e = V_DIM
    hps = q_ref.shape[1] // e
    heads = range(hps)
    qi = pl.program_id(2)

    @pl.when(qi == 0)
    def _():
        for h in heads:
            vt_sc[h] = v_ref[:, h * e:(h + 1) * e].T

    feat = lax.broadcasted_iota(jnp.int32, (e, tq), 0)
    qst = []
    for h in heads:
        qt = q_ref[:, h * e:(h + 1) * e].T
        zero = jnp.zeros_like(qt)
        qst.append(jnp.concatenate([jnp.where(feat < HEAD_DIM, qt, zero), jnp.where(feat >= HEAD_DIM, qt, zero)],
                                   axis=1))
    acc_sc[...] = jnp.zeros(acc_sc.shape, F32)

    def step(j, carry, masked):
        start = pl.multiple_of(j * tq, tq)
        out = []
        for h in heads:
            m_prev, l_prev = carry[h]
            kb = k_ref[pl.ds(start, tq), h * e:(h + 1) * e]
            vt = vt_sc[h, :, pl.ds(start, tq)]
            st = jnp.dot(kb, qst[h], preferred_element_type=F32)
            if masked:
                key = lax.broadcasted_iota(jnp.int32, st.shape, 0)
                col = lax.broadcasted_iota(jnp.int32, st.shape, 1)
                qpos = jnp.where(col >= tq, col - tq, col)
                st = jnp.where(key <= qpos, st, NEG_INIT)
            m_new = jnp.maximum(m_prev, jnp.max(st, axis=0, keepdims=True))
            alpha = jnp.exp2(m_prev - m_new)
            p = jnp.exp2(st - m_new)
            l_new = l_prev * alpha + jnp.sum(p, axis=0, keepdims=True)
            acc_sc[h] = acc_sc[h] * alpha + jnp.dot(vt, p.astype(BF16), preferred_element_type=F32)
            out.append((m_new, l_new))
        return tuple(out)

    def pair_body(jj, carry):
        carry = step(2 * jj, carry, False)
        return step(2 * jj + 1, carry, False)

    init = tuple((jnp.full((1, 2 * tq), NEG_INIT, F32), jnp.zeros((1, 2 * tq), F32)) for _ in heads)
    carry = lax.fori_loop(0, qi // 2, pair_body, init)
    carry = lax.cond(qi % 2 == 1, lambda c: step(qi - 1, c, False), lambda c: c, carry)
    carry = step(qi, carry, True)

    lam = _lambda_value(lq1_ref[...], lk1_ref[...], lq2_ref[...], lk2_ref[...])
    for h in heads:
        ot = acc_sc[h] / carry[h][1]
        dlt = ot[:, :tq] - lam * ot[:, tq:]
        ms = jnp.mean(dlt * dlt, axis=0, keepdims=True)
        r = (dlt * lax.rsqrt(ms + SUBLN_EPS)).T
        cols = slice(h * e, (h + 1) * e)
        o_ref[:, cols] = (r * sg_ref[...] * (1.0 - LAMBDA_INIT) * ag_ref[:, cols].astype(F32)).astype(BF16)


def _prompt_attention(q3, k3, v3, ag3, lq1, lk1, lq2, lk2, subln_g, tq, hps):
    b, s, d = q3.shape
    lam_spec = pl.BlockSpec((1, HEAD_DIM), lambda i, h, j: (0, 0))
    qtile = pl.BlockSpec((None, tq, hps * V_DIM), lambda i, h, j: (i, j, h))
    kv = pl.BlockSpec((None, s, hps * V_DIM), lambda i, h, j: (i, 0, h))
    return pl.pallas_call(
        _prompt_attn_body,
        grid=(b, N_HEADS // hps, s // tq),
        in_specs=[lam_spec, lam_spec, lam_spec, lam_spec,
                  pl.BlockSpec((1, V_DIM), lambda i, h, j: (0, 0)),
                  qtile, kv, kv, qtile],
        out_specs=qtile,
        out_shape=jax.ShapeDtypeStruct((b, s, d), BF16),
        scratch_shapes=[pltpu.VMEM((hps, V_DIM, s), BF16), pltpu.VMEM((hps, V_DIM, 2 * tq), F32)],
        compiler_params=pltpu.CompilerParams(dimension_semantics=("parallel", "parallel", "arbitrary"),
                                             vmem_limit_bytes=V7X_VMEM_LIMIT),
        name="prompt_attention",
    )(lq1.reshape(1, -1), lk1.reshape(1, -1), lq2.reshape(1, -1), lk2.reshape(1, -1),
      subln_g.reshape(1, -1), q3, k3, v3, ag3)


def _paged_attn_body(n_pages_per_step, pt_ref, wt_ref, knew_ref, vnew_ref, *rest):
    del pt_ref
    pps = n_pages_per_step
    nh = N_HEADS
    k_refs = rest[:pps]
    v_refs = rest[pps:2 * pps]
    o_ref, m_sc, l_sc, acc_sc, w_sc, kc_sc, vc_sc = rest[2 * pps:]
    rows, e = k_refs[0].shape
    page = rows // nh
    ncol = wt_ref.shape[0]
    t_new = knew_ref.shape[0] // nh
    cols_per_head = ncol // nh
    c = pl.program_id(1)

    @pl.when(c == 0)
    def _():
        m_sc[...] = jnp.full(m_sc.shape, NEG_INIT, F32)
        l_sc[...] = jnp.zeros(l_sc.shape, F32)
        acc_sc[...] = jnp.zeros(acc_sc.shape, F32)
        wt = wt_ref[...]
        r = lax.broadcasted_iota(jnp.int32, (ncol, e), 0)
        f = lax.broadcasted_iota(jnp.int32, (ncol, e), 1)
        same_map = ((r // t_new) % 2) == (f // HEAD_DIM)
        wmt = jnp.where(same_map, wt, jnp.zeros_like(wt)).T
        col_head = lax.broadcasted_iota(jnp.int32, (e, ncol), 1) // cols_per_head
        for h in range(nh):
            w_sc[h * e:(h + 1) * e, :] = jnp.where(col_head == h, wmt, jnp.zeros_like(wmt))

    def regroup(src_ref, n):
        return [src_ref[pl.ds(h, n, stride=nh), :] for h in range(nh)]

    for i in range(pps):
        for h, (kh, vh) in enumerate(zip(regroup(k_refs[i], page), regroup(v_refs[i], page))):
            kc_sc[i * page:(i + 1) * page, h * e:(h + 1) * e] = kh.astype(BF16)
            vc_sc[i * page:(i + 1) * page, h * e:(h + 1) * e] = vh.astype(BF16)

    def as_column(row_vec):
        return jnp.broadcast_to(row_vec, (e, ncol)).T

    def update(carry, kc, vc, causal, n_split):
        m_prev, l_prev, acc = carry
        n = kc.shape[0]
        rb = n // n_split
        st = jnp.concatenate([jnp.dot(kc[i * rb:(i + 1) * rb], w_sc[...], preferred_element_type=F32)
                              for i in range(n_split)], axis=0)
        if causal:
            key_pos = lax.broadcasted_iota(jnp.int32, st.shape, 0)
            q_pos = lax.broadcasted_iota(jnp.int32, st.shape, 1) % t_new
            st = jnp.where(key_pos <= q_pos, st, NEG_INIT)
        m_new = jnp.maximum(m_prev, jnp.max(st, axis=0, keepdims=True))
        alpha = jnp.exp2(m_prev - m_new)
        p = jnp.exp2(st - m_new)
        l_new = l_prev * alpha + jnp.sum(p, axis=0, keepdims=True)
        pb = p.astype(BF16)
        hg = nh // n_split
        blocks = []
        for g in range(n_split):
            pv_g = lax.dot_general(pb, vc[:, g * hg * e:(g + 1) * hg * e], _TN,
                                   preferred_element_type=F32)
            for j in range(hg):
                h = g * hg + j
                blocks.append(pv_g[h * cols_per_head:(h + 1) * cols_per_head, j * e:(j + 1) * e])
        pv = jnp.concatenate(blocks, axis=0)
        return m_new, l_new, acc * as_column(alpha) + pv

    carry = update((m_sc[...], l_sc[...], acc_sc[...]), kc_sc[...], vc_sc[...], False, 2)
    m_sc[...], l_sc[...], acc_sc[...] = carry

    @pl.when(c == pl.num_programs(1) - 1)
    def _():
        pad = jnp.zeros((t_new, nh * e), F32)
        kn = jnp.concatenate([jnp.concatenate(regroup(knew_ref, t_new), axis=1), pad], axis=0).astype(BF16)
        vn = jnp.concatenate([jnp.concatenate(regroup(vnew_ref, t_new), axis=1), pad], axis=0).astype(BF16)
        _, l_fin, acc = update(carry, kn, vn, True, 1)
        o_ref[...] = acc / as_column(l_fin)


def _paged_attention(page_table, wt, cache_k, cache_v, k_new, v_new, pps):
    nb, n_pages = page_table.shape
    _, rows, e = cache_k.shape
    ncol = wt.shape[1]
    new_rows = k_new.shape[1]

    def page_spec(i):
        return pl.BlockSpec((None, rows, e), lambda b, c, pt: (pt[b, c * pps + i], 0, 0))

    new_spec = pl.BlockSpec((None, new_rows, e), lambda b, c, pt: (b, 0, 0))
    grid_spec = pltpu.PrefetchScalarGridSpec(
        num_scalar_prefetch=1,
        grid=(nb, n_pages // pps),
        in_specs=[pl.BlockSpec((None, ncol, e), lambda b, c, pt: (b, 0, 0)), new_spec, new_spec]
                 + [page_spec(i) for i in range(pps)] * 2,
        out_specs=pl.BlockSpec((None, ncol, e), lambda b, c, pt: (b, 0, 0)),
        scratch_shapes=[pltpu.VMEM((1, ncol), F32), pltpu.VMEM((1, ncol), F32), pltpu.VMEM((ncol, e), F32),
                        pltpu.VMEM((N_HEADS * e, ncol), BF16),
                        pltpu.VMEM((pps * rows // N_HEADS, N_HEADS * e), BF16),
                        pltpu.VMEM((pps * rows // N_HEADS, N_HEADS * e), BF16)],
    )
    return pl.pallas_call(
        functools.partial(_paged_attn_body, pps),
        grid_spec=grid_spec,
        out_shape=jax.ShapeDtypeStruct((nb, ncol, e), F32),
        compiler_params=pltpu.CompilerParams(dimension_semantics=("parallel", "arbitrary"),
                                             vmem_limit_bytes=V7X_VMEM_LIMIT),
        name="paged_attention",
    )(page_table, wt, k_new, v_new, *([cache_k] * pps), *([cache_v] * pps))


def _sample_post_body(lq1_ref, lk1_ref, lq2_ref, lk2_ref, sg_ref, o1_ref, o2_ref, ag_ref, og_ref):
    lam = _lambda_value(lq1_ref[...], lk1_ref[...], lq2_ref[...], lk2_ref[...])
    for h in range(N_HEADS):
        cols = slice(h * V_DIM, (h + 1) * V_DIM)
        og_ref[:, cols] = _head_post(o1_ref[:, cols], o2_ref[:, cols], lam, sg_ref[...], ag_ref[:, cols])


def _sample_post(o1, o2, ag, lq1, lk1, lq2, lk2, subln_g):
    m, d = o1.shape
    full = pl.BlockSpec((m, d), lambda i: (0, 0))
    lam_spec = pl.BlockSpec((1, HEAD_DIM), lambda i: (0, 0))
    return pl.pallas_call(
        _sample_post_body,
        grid=(1,),
        in_specs=[lam_spec, lam_spec, lam_spec, lam_spec, pl.BlockSpec((1, V_DIM), lambda i: (0, 0)),
                  full, full, full],
        out_specs=full,
        out_shape=jax.ShapeDtypeStruct((m, d), BF16),
        compiler_params=pltpu.CompilerParams(vmem_limit_bytes=V7X_VMEM_LIMIT),
        name="sample_attention_post",
    )(lq1.reshape(1, -1), lk1.reshape(1, -1), lq2.reshape(1, -1), lk2.reshape(1, -1),
      subln_g.reshape(1, -1), o1, o2, ag)


def _out_body(og_ref, gc_ref, sa_ref, x_ref, wap_ref, wo_ref, gf_ref, y_ref):
    ya = jnp.dot(og_ref[...], wap_ref[...], preferred_element_type=F32)
    merged = gc_ref[...].astype(F32) + sa_ref[...].astype(F32) * ya
    y = x_ref[...] + jnp.dot(merged.astype(BF16), wo_ref[...], preferred_element_type=F32)
    ms = jnp.mean(y * y, axis=-1, keepdims=True)
    y_ref[...] = y * lax.rsqrt(ms + RMS_EPS) * gf_ref[...]


def _out_projection(og, gc, sa, x2, w_attn_bf, w_out_bf, final_g, tm):
    m, d = x2.shape
    row = pl.BlockSpec((tm, d), lambda i: (i, 0))
    wspec = pl.BlockSpec((d, d), lambda i: (0, 0))
    return pl.pallas_call(
        _out_body,
        grid=(m // tm,),
        in_specs=[row, row, row, row, wspec, wspec, pl.BlockSpec((1, d), lambda i: (0, 0))],
        out_specs=row,
        out_shape=jax.ShapeDtypeStruct((m, d), F32),
        compiler_params=pltpu.CompilerParams(dimension_semantics=("parallel",),
                                             vmem_limit_bytes=V7X_VMEM_LIMIT),
        name="out_projection",
    )(og, gc, sa, x2, w_attn_bf, w_out_bf, final_g.reshape(1, d))


def kernel(x_prompt, x_sample, cache_k, cache_v, state_conv, page_table, norm_g, w_in, conv_w, conv_b,
           conv_ln_g, conv_ln_b, w_conv_proj, lambda_q1, lambda_k1, lambda_q2, lambda_k2, subln_g,
           w_attn_proj, w_out, final_norm_g):
    depth = norm_g.shape[0]
    assert depth == 1, "single-layer step"
    b, s, d = x_prompt.shape
    nb, t_new, _ = x_sample.shape
    w_in_bf = w_in[0].astype(BF16)
    w_conv_bf = w_conv_proj[0].astype(BF16)
    w_attn_bf = w_attn_proj[0].astype(BF16)
    w_out_bf = w_out[0].astype(BF16)
    lams = (lambda_q1[0], lambda_k1[0], lambda_q2[0], lambda_k2[0])

    def conv_and_state(u, cg, sc, buf, bsz, seq, bb, ts):
        buf32 = jnp.pad(buf, ((0, 0), (HALO_PAD, 0), (0, 0)))
        gc, nbuf = _conv_branch(u.reshape(bsz, seq, d), cg.reshape(bsz, seq, d), sc.reshape(bsz, seq, d), buf32,
                                conv_w[0], conv_b[0], conv_ln_g[0], conv_ln_b[0], w_conv_bf, bb, ts)
        return gc.reshape(bsz * seq, d), nbuf[:, HALO_PAD:, :]

    xp2 = x_prompt.reshape(b * s, d)
    u, cg, q, k32, kb, v32, vb, ag, sc, sa = _in_projection(xp2, norm_g[0], w_in_bf, 256)
    gc, conv_prompt = conv_and_state(u, cg, sc, jnp.zeros((b, CONV_WIDTH - 1, d), F32), b, s, 1, 256)
    og = _prompt_attention(q.reshape(b, s, d), kb.reshape(b, s, d), vb.reshape(b, s, d), ag.reshape(b, s, d),
                           *lams, subln_g[0], 256, 2)
    y_prompt = _out_projection(og.reshape(b * s, d), gc, sa, xp2, w_attn_bf, w_out_bf, final_norm_g, 512)
    k_prompt = k32.reshape(1, b, s, N_HEADS, V_DIM)
    v_prompt = v32.reshape(1, b, s, N_HEADS, V_DIM)

    xs2 = x_sample.reshape(nb * t_new, d)
    u, cg, q, k32, kb, v32, vb, ag, sc, sa = _in_projection(xs2, norm_g[0], w_in_bf, 256)
    gc, conv_sample = conv_and_state(u, cg, sc, state_conv[0], nb, t_new, 8, t_new)
    k_sample = k32.reshape(nb, t_new, N_HEADS, V_DIM)
    v_sample = v32.reshape(nb, t_new, N_HEADS, V_DIM)
    qh = q.reshape(nb, t_new, N_HEADS, V_DIM).transpose(0, 2, 1, 3)
    wt = jnp.broadcast_to(qh[:, :, None], (nb, N_HEADS, 2, t_new, V_DIM)).reshape(nb, N_HEADS * 2 * t_new, V_DIM)
    n_pool, page = cache_k.shape[1], cache_k.shape[2]
    rows_kv = (page * N_HEADS, V_DIM)
    on = _paged_attention(page_table, wt, cache_k.reshape(n_pool, *rows_kv), cache_v.reshape(n_pool, *rows_kv),
                          k32.reshape(nb, t_new * N_HEADS, V_DIM), v32.reshape(nb, t_new * N_HEADS, V_DIM),
                          8)
    o5 = on.reshape(nb, N_HEADS, 2, t_new, V_DIM).transpose(2, 0, 3, 1, 4)
    o1 = o5[0].reshape(nb * t_new, d)
    o2 = o5[1].reshape(nb * t_new, d)
    og = _sample_post(o1, o2, ag, *lams, subln_g[0])
    y_sample = _out_projection(og, gc, sa, xs2, w_attn_bf, w_out_bf, final_norm_g, 256)

    return (y_prompt.reshape(b, s, d), y_sample.reshape(nb, t_new, d), k_prompt, v_prompt, conv_prompt[None],
            k_sample[None], v_sample[None], conv_sample[None])
```

```python
import math

import jax
import jax.numpy as jnp
from jax import lax
from jax.experimental import pallas as pl
from jax.experimental.pallas import tpu as pltpu

F32 = jnp.float32
BF16 = jnp.bfloat16

N_HEADS = 8
HEAD_DIM = 64
V_DIM = 2 * HEAD_DIM
CONV_WIDTH = 31
CONV_HALO = 32
HALO_PAD = CONV_HALO - (CONV_WIDTH - 1)
SUBLANES = 8
CONV_ROW_CHUNK = 64
CONV_COL_CHUNK = 256
RMS_EPS = 1e-6
SUBLN_EPS = 1e-5
LN_EPS = 1e-5
NEG_INIT = -1e30
LAMBDA_INIT = 0.8 - 0.6 * math.exp(-0.3 * 0)
Q_SCALE = HEAD_DIM ** -0.5 * math.log2(math.e)
N_SPLITS = 9
V7X_VMEM_LIMIT = 56 * 1024 * 1024

_NT = (((1,), (1,)), ((), ()))
_TN = (((0,), (0,)), ((), ()))


def _sigmoid(z):
    return 1.0 / (1.0 + jnp.exp(-z))


def _silu(z):
    return z * _sigmoid(z)


def _lambda_value(lq1, lk1, lq2, lk2):
    a = jnp.sum(lq1 * lk1, axis=-1, keepdims=True)
    b = jnp.sum(lq2 * lk2, axis=-1, keepdims=True)
    return jnp.exp(a) - jnp.exp(b) + LAMBDA_INIT


def _inproj_body(x_ref, g_ref, w_ref, u_ref, cg_ref, q_ref, k32_ref, kb_ref, v32_ref, vb_ref,
                 ag_ref, sc_ref, sa_ref):
    d = x_ref.shape[-1]
    x = x_ref[...]
    ms = jnp.mean(x * x, axis=-1, keepdims=True)
    h = (x * lax.rsqrt(ms + RMS_EPS) * g_ref[...]).astype(BF16)

    def proj(i):
        return jnp.dot(h, w_ref[:, i * d:(i + 1) * d], preferred_element_type=F32)

    u_ref[...] = (proj(0) * _sigmoid(proj(1))).astype(BF16)
    cg_ref[...] = _silu(proj(2)).astype(BF16)
    q_ref[...] = (proj(3) * Q_SCALE).astype(BF16)
    k = proj(4)
    k32_ref[...] = k
    kb_ref[...] = k.astype(BF16)
    v = proj(5)
    v32_ref[...] = v
    vb_ref[...] = v.astype(BF16)
    ag_ref[...] = _silu(proj(6)).astype(BF16)
    sc_ref[...] = _sigmoid(proj(7)).astype(BF16)
    sa_ref[...] = _sigmoid(proj(8)).astype(BF16)


def _in_projection(x2, norm_g, w_in_bf, tm):
    m, d = x2.shape
    d_in = w_in_bf.shape[1]
    row = pl.BlockSpec((tm, d), lambda i: (i, 0))
    out_dtypes = [BF16, BF16, BF16, F32, BF16, F32, BF16, BF16, BF16, BF16]
    return pl.pallas_call(
        _inproj_body,
        grid=(m // tm,),
        in_specs=[row,
                  pl.BlockSpec((1, d), lambda i: (0, 0)),
                  pl.BlockSpec((d, d_in), lambda i: (0, 0), pipeline_mode=pl.Buffered(1))],
        out_specs=[row] * len(out_dtypes),
        out_shape=[jax.ShapeDtypeStruct((m, d), t) for t in out_dtypes],
        compiler_params=pltpu.CompilerParams(dimension_semantics=("parallel",),
                                             vmem_limit_bytes=V7X_VMEM_LIMIT),
        name="in_projection",
    )(x2, norm_g.reshape(1, d), w_in_bf)


def _conv_body(u_ref, cg_ref, sc_ref, buf_ref, cw_ref, cb_ref, lg_ref, lb_ref, w_ref,
               gc_ref, nbuf_ref, full_ref, sh_ref, conv_ref):
    bb, ts, d = u_ref.shape
    s = pl.program_id(1)

    @pl.when(s == 0)
    def _():
        full_ref[:, 0:CONV_HALO, :] = buf_ref[...]

    full_ref[:, CONV_HALO:CONV_HALO + ts, :] = u_ref[...].astype(F32)
    n_sh = sh_ref.shape[2]
    for r in range(1, SUBLANES):
        sh_ref[r - 1] = full_ref[:, r:r + n_sh, :]

    rc = min(CONV_ROW_CHUNK, ts)

    def chunk(ci, carry):
        base = pl.multiple_of(ci * rc, rc)
        for bi in range(bb):
            for c0 in range(0, d, CONV_COL_CHUNK):
                cols = slice(c0, c0 + CONV_COL_CHUNK)
                acc = jnp.broadcast_to(cb_ref[:, cols], (rc, CONV_COL_CHUNK))
                for j in range(CONV_WIDTH):
                    r = (HALO_PAD + j) % SUBLANES
                    a = HALO_PAD + j - r
                    src = full_ref if r == 0 else sh_ref.at[r - 1]
                    acc = acc + cw_ref[j:j + 1, cols] * src[bi, pl.ds(base + a, rc), cols]
                conv_ref[bi, pl.ds(base, rc), cols] = acc
        return carry

    lax.fori_loop(0, ts // rc, chunk, 0)
    acc = conv_ref[...]
    mu = jnp.mean(acc, axis=-1, keepdims=True)
    cen = acc - mu
    var = jnp.mean(cen * cen, axis=-1, keepdims=True)
    y = cen * lax.rsqrt(var + LN_EPS) * lg_ref[...].reshape(1, 1, d) + lb_ref[...].reshape(1, 1, d)
    c = _silu(y) * cg_ref[...].astype(F32)
    yc = jnp.dot(c.astype(BF16).reshape(bb * ts, d), w_ref[...], preferred_element_type=F32)
    gc_ref[...] = (sc_ref[...].astype(F32) * yc.reshape(bb, ts, d)).astype(BF16)
    tail = full_ref[:, ts:ts + CONV_HALO, :]
    full_ref[:, 0:CONV_HALO, :] = tail
    nbuf_ref[...] = tail


def _conv_branch(u3, cg3, sc3, buf32, conv_w, conv_b, ln_g, ln_b, w_proj_bf, bb, ts):
    b, s, d = u3.shape
    tile = pl.BlockSpec((bb, ts, d), lambda i, j: (i, j, 0))
    halo = pl.BlockSpec((bb, CONV_HALO, d), lambda i, j: (i, 0, 0))
    vec = pl.BlockSpec((1, d), lambda i, j: (0, 0))
    return pl.pallas_call(
        _conv_body,
        grid=(b // bb, s // ts),
        in_specs=[tile, tile, tile, halo,
                  pl.BlockSpec((CONV_WIDTH, d), lambda i, j: (0, 0)),
                  vec, vec, vec,
                  pl.BlockSpec((d, d), lambda i, j: (0, 0))],
        out_specs=[tile, halo],
        out_shape=[jax.ShapeDtypeStruct((b, s, d), BF16),
                   jax.ShapeDtypeStruct((b, CONV_HALO, d), F32)],
        scratch_shapes=[pltpu.VMEM((bb, CONV_HALO + ts, d), F32),
                        pltpu.VMEM((SUBLANES - 1, bb, CONV_HALO - SUBLANES + ts, d), F32),
                        pltpu.VMEM((bb, ts, d), F32)],
        compiler_params=pltpu.CompilerParams(dimension_semantics=("parallel", "arbitrary"),
                                             vmem_limit_bytes=V7X_VMEM_LIMIT),
        name="conv_branch",
    )(u3, cg3, sc3, buf32, conv_w, conv_b.reshape(1, d), ln_g.reshape(1, d), ln_b.reshape(1, d), w_proj_bf)


def _head_post(o1, o2, lam, sg, ag):
    dlt = o1 - lam * o2
    ms = jnp.mean(dlt * dlt, axis=-1, keepdims=True)
    r = dlt * lax.rsqrt(ms + SUBLN_EPS) * sg * (1.0 - LAMBDA_INIT)
    return (r * ag.astype(F32)).astype(BF16)


def _prompt_attn_body(lq1_ref, lk1_ref, lq2_ref, lk2_ref, sg_ref, q_ref, k_ref, v_ref, ag_ref,
                      o_ref, vt_sc, acc_sc):
    tq = q_ref.shape[0]
    e = V_DIM
    hps = q_ref.shape[1] // e
    heads = range(hps)
    qi = pl.program_id(2)

    @pl.when(qi == 0)
    def _():
        for h in heads:
            vt_sc[h] = v_ref[:, h * e:(h + 1) * e].T

    feat = lax.broadcasted_iota(jnp.int32, (e, tq), 0)
    qst = []
    for h in heads:
        qt = q_ref[:, h * e:(h + 1) * e].T
        zero = jnp.zeros_like(qt)
        qst.append(jnp.concatenate([jnp.where(feat < HEAD_DIM, qt, zero), jnp.where(feat >= HEAD_DIM, qt, zero)],
                                   axis=1))
    acc_sc[...] = jnp.zeros(acc_sc.shape, F32)

    def step(j, carry, masked):
        start = pl.multiple_of(j * tq, tq)
        out = []
        for h in heads:
            m_prev, l_prev = carry[h]
            kb = k_ref[pl.ds(start, tq), h * e:(h + 1) * e]
            vt = vt_sc[h, :, pl.ds(start, tq)]
            st = jnp.dot(kb, qst[h], preferred_element_type=F32)
            if masked:
                key = lax.broadcasted_iota(jnp.int32, st.shape, 0)
                col = lax.broadcasted_iota(jnp.int32, st.shape, 1)
                qpos = jnp.where(col >= tq, col - tq, col)
                st = jnp.where(key <= qpos, st, NEG_INIT)
            m_new = jnp.maximum(m_prev, jnp.max(st, axis=0, keepdims=True))
            alpha = jnp.exp2(m_prev - m_new)
            p = jnp.exp2(st - m_new)
            l_new = l_prev * alpha + jnp.sum(p, axis=0, keepdims=True)
            acc_sc[h] = acc_sc[h] * alpha + jnp.dot(vt, p.astype(BF16), preferred_element_type=F32)
            out.append((m_new, l_new))
        return tuple(out)

    def pair_body(jj, carry):
        carry = step(2 * jj, carry, False)
        return step(2 * jj + 1, carry, False)

    init = tuple((jnp.full((1, 2 * tq), NEG_INIT, F32), jnp.zeros((1, 2 * tq), F32)) for _ in heads)
    carry = lax.fori_loop(0, qi // 2, pair_body, init)
    carry = lax.cond(qi % 2 == 1, lambda c: step(qi - 1, c, False), lambda c: c, carry)
    carry = step(qi, carry, True)

    lam = _lambda_value(lq1_ref[...], lk1_ref[...], lq2_ref[...], lk2_ref[...])
    for h in heads:
        ot = acc_sc[h] / carry[h][1]
        dlt = ot[:, :tq] - lam * ot[:, tq:]
        ms = jnp.mean(dlt * dlt, axis=0, keepdims=True)
        r = (dlt * lax.rsqrt(ms + SUBLN_EPS)).T
        cols = slice(h * e, (h + 1) * e)
        o_ref[:, cols] = (r * sg_ref[...] * (1.0 - LAMBDA_INIT) * ag_ref[:, cols].astype(F32)).astype(BF16)


def _prompt_attention(q3, k3, v3, ag3, lq1, lk1, lq2, lk2, subln_g, tq, hps):
    b, s, d = q3.shape
    lam_spec = pl.BlockSpec((1, HEAD_DIM), lambda i, h, j: (0, 0))
    qtile = pl.BlockSpec((None, tq, hps * V_DIM), lambda i, h, j: (i, j, h))
    kv = pl.BlockSpec((None, s, hps * V_DIM), lambda i, h, j: (i, 0, h))
    return pl.pallas_call(
        _prompt_attn_body,
        grid=(b, N_HEADS // hps, s // tq),
        in_specs=[lam_spec, lam_spec, lam_spec, lam_spec,
                  pl.BlockSpec((1, V_DIM), lambda i, h, j: (0, 0)),
                  qtile, kv, kv, qtile],
        out_specs=qtile,
        out_shape=jax.ShapeDtypeStruct((b, s, d), BF16),
        scratch_shapes=[pltpu.VMEM((hps, V_DIM, s), BF16), pltpu.VMEM((hps, V_DIM, 2 * tq), F32)],
        compiler_params=pltpu.CompilerParams(dimension_semantics=("parallel", "parallel", "arbitrary"),
                                             vmem_limit_bytes=V7X_VMEM_LIMIT),
        name="prompt_attention",
    )(lq1.reshape(1, -1), lk1.reshape(1, -1), lq2.reshape(1, -1), lk2.reshape(1, -1),
      subln_g.reshape(1, -1), q3, k3, v3, ag3)


PAGE_RING_DEPTH = 3


def _paged_attn_body(pt_ref, wt_ref, knew_ref, vnew_ref, ck_hbm, cv_hbm,
                     o_ref, m_sc, l_sc, acc_sc, w_sc, kc_sc, vc_sc, kbuf, vbuf, sem):
    nh = N_HEADS
    _, pps, rows, e = kbuf.shape
    page = rows // nh
    ncol = wt_ref.shape[0]
    t_new = knew_ref.shape[0] // nh
    cols_per_head = ncol // nh
    c = pl.program_id(1)
    n_c = pl.num_programs(1)
    g = pl.program_id(0) * n_c + c
    n_steps = pl.num_programs(0) * n_c

    def page_copies(step):
        slot = lax.rem(step, PAGE_RING_DEPTH)
        sb = lax.div(step, n_c)
        sc = lax.rem(step, n_c)
        copies = []
        for i in range(pps):
            pg = pt_ref[sb, sc * pps + i]
            copies.append(pltpu.make_async_copy(ck_hbm.at[pg], kbuf.at[slot, i], sem.at[slot, 0, i]))
            copies.append(pltpu.make_async_copy(cv_hbm.at[pg], vbuf.at[slot, i], sem.at[slot, 1, i]))
        return copies

    @pl.when(g == 0)
    def _():
        for s in range(PAGE_RING_DEPTH - 1):
            for cp in page_copies(jnp.int32(s)):
                cp.start()

    @pl.when(g + (PAGE_RING_DEPTH - 1) < n_steps)
    def _():
        for cp in page_copies(g + (PAGE_RING_DEPTH - 1)):
            cp.start()

    for cp in page_copies(g):
        cp.wait()
    slot = lax.rem(g, PAGE_RING_DEPTH)

    @pl.when(c == 0)
    def _():
        m_sc[...] = jnp.full(m_sc.shape, NEG_INIT, F32)
        l_sc[...] = jnp.zeros(l_sc.shape, F32)
        acc_sc[...] = jnp.zeros(acc_sc.shape, F32)
        wt = wt_ref[...]
        r = lax.broadcasted_iota(jnp.int32, (ncol, e), 0)
        f = lax.broadcasted_iota(jnp.int32, (ncol, e), 1)
        same_map = ((r // t_new) % 2) == (f // HEAD_DIM)
        wmt = jnp.where(same_map, wt, jnp.zeros_like(wt)).T
        col_head = lax.broadcasted_iota(jnp.int32, (e, ncol), 1) // cols_per_head
        for h in range(nh):
            w_sc[h * e:(h + 1) * e, :] = jnp.where(col_head == h, wmt, jnp.zeros_like(wmt))

    def regroup(src_ref, n):
        return [src_ref[pl.ds(h, n, stride=nh), :] for h in range(nh)]

    for i in range(pps):
        for h, (kh, vh) in enumerate(zip(regroup(kbuf.at[slot, i], page), regroup(vbuf.at[slot, i], page))):
            kc_sc[i * page:(i + 1) * page, h * e:(h + 1) * e] = kh.astype(BF16)
            vc_sc[i * page:(i + 1) * page, h * e:(h + 1) * e] = vh.astype(BF16)

    def as_column(row_vec):
        return jnp.broadcast_to(row_vec, (e, ncol)).T

    def update(carry, kc, vc, causal, n_split):
        m_prev, l_prev, acc = carry
        n = kc.shape[0]
        rb = n // n_split
        st = jnp.concatenate([jnp.dot(kc[i * rb:(i + 1) * rb], w_sc[...], preferred_element_type=F32)
                              for i in range(n_split)], axis=0)
        if causal:
            key_pos = lax.broadcasted_iota(jnp.int32, st.shape, 0)
            q_pos = lax.broadcasted_iota(jnp.int32, st.shape, 1) % t_new
            st = jnp.where(key_pos <= q_pos, st, NEG_INIT)
        m_new = jnp.maximum(m_prev, jnp.max(st, axis=0, keepdims=True))
        alpha = jnp.exp2(m_prev - m_new)
        p = jnp.exp2(st - m_new)
        l_new = l_prev * alpha + jnp.sum(p, axis=0, keepdims=True)
        pb = p.astype(BF16)
        hg = nh // n_split
        blocks = []
        for g in range(n_split):
            pv_g = lax.dot_general(pb, vc[:, g * hg * e:(g + 1) * hg * e], _TN,
                                   preferred_element_type=F32)
            for j in range(hg):
                h = g * hg + j
                blocks.append(pv_g[h * cols_per_head:(h + 1) * cols_per_head, j * e:(j + 1) * e])
        pv = jnp.concatenate(blocks, axis=0)
        return m_new, l_new, acc * as_column(alpha) + pv

    carry = update((m_sc[...], l_sc[...], acc_sc[...]), kc_sc[...], vc_sc[...], False, 2)
    m_sc[...], l_sc[...], acc_sc[...] = carry

    @pl.when(c == pl.num_programs(1) - 1)
    def _():
        pad = jnp.zeros((t_new, nh * e), F32)
        kn = jnp.concatenate([jnp.concatenate(regroup(knew_ref, t_new), axis=1), pad], axis=0).astype(BF16)
        vn = jnp.concatenate([jnp.concatenate(regroup(vnew_ref, t_new), axis=1), pad], axis=0).astype(BF16)
        _, l_fin, acc = update(carry, kn, vn, True, 1)
        o_ref[...] = acc / as_column(l_fin)


def _paged_attention(page_table, wt, cache_k, cache_v, k_new, v_new, pps):
    nb, n_pages = page_table.shape
    _, rows, e = cache_k.shape
    ncol = wt.shape[1]
    new_rows = k_new.shape[1]

    assert nb * (n_pages // pps) >= PAGE_RING_DEPTH - 1
    new_spec = pl.BlockSpec((None, new_rows, e), lambda b, c, pt: (b, 0, 0))
    hbm = pl.BlockSpec(memory_space=pl.ANY)
    grid_spec = pltpu.PrefetchScalarGridSpec(
        num_scalar_prefetch=1,
        grid=(nb, n_pages // pps),
        in_specs=[pl.BlockSpec((None, ncol, e), lambda b, c, pt: (b, 0, 0)), new_spec, new_spec, hbm, hbm],
        out_specs=pl.BlockSpec((None, ncol, e), lambda b, c, pt: (b, 0, 0)),
        scratch_shapes=[pltpu.VMEM((1, ncol), F32), pltpu.VMEM((1, ncol), F32), pltpu.VMEM((ncol, e), F32),
                        pltpu.VMEM((N_HEADS * e, ncol), BF16),
                        pltpu.VMEM((pps * rows // N_HEADS, N_HEADS * e), BF16),
                        pltpu.VMEM((pps * rows // N_HEADS, N_HEADS * e), BF16),
                        pltpu.VMEM((PAGE_RING_DEPTH, pps, rows, e), F32),
                        pltpu.VMEM((PAGE_RING_DEPTH, pps, rows, e), F32),
                        pltpu.SemaphoreType.DMA((PAGE_RING_DEPTH, 2, pps))],
    )
    return pl.pallas_call(
        _paged_attn_body,
        grid_spec=grid_spec,
        out_shape=jax.ShapeDtypeStruct((nb, ncol, e), F32),
        compiler_params=pltpu.CompilerParams(dimension_semantics=("arbitrary", "arbitrary"),
                                             vmem_limit_bytes=V7X_VMEM_LIMIT),
        name="paged_attention",
    )(page_table, wt, k_new, v_new, cache_k, cache_v)


def _sample_post_body(lq1_ref, lk1_ref, lq2_ref, lk2_ref, sg_ref, o1_ref, o2_ref, ag_ref, og_ref):
    lam = _lambda_value(lq1_ref[...], lk1_ref[...], lq2_ref[...], lk2_ref[...])
    for h in range(N_HEADS):
        cols = slice(h * V_DIM, (h + 1) * V_DIM)
        og_ref[:, cols] = _head_post(o1_ref[:, cols], o2_ref[:, cols], lam, sg_ref[...], ag_ref[:, cols])


def _sample_post(o1, o2, ag, lq1, lk1, lq2, lk2, subln_g):
    m, d = o1.shape
    full = pl.BlockSpec((m, d), lambda i: (0, 0))
    lam_spec = pl.BlockSpec((1, HEAD_DIM), lambda i: (0, 0))
    return pl.pallas_call(
        _sample_post_body,
        grid=(1,),
        in_specs=[lam_spec, lam_spec, lam_spec, lam_spec, pl.BlockSpec((1, V_DIM), lambda i: (0, 0)),
                  full, full, full],
        out_specs=full,
        out_shape=jax.ShapeDtypeStruct((m, d), BF16),
        compiler_params=pltpu.CompilerParams(vmem_limit_bytes=V7X_VMEM_LIMIT),
        name="sample_attention_post",
    )(lq1.reshape(1, -1), lk1.reshape(1, -1), lq2.reshape(1, -1), lk2.reshape(1, -1),
      subln_g.reshape(1, -1), o1, o2, ag)


def _out_body(og_ref, gc_ref, sa_ref, x_ref, wap_ref, wo_ref, gf_ref, y_ref):
    ya = jnp.dot(og_ref[...], wap_ref[...], preferred_element_type=F32)
    merged = gc_ref[...].astype(F32) + sa_ref[...].astype(F32) * ya
    y = x_ref[...] + jnp.dot(merged.astype(BF16), wo_ref[...], preferred_element_type=F32)
    ms = jnp.mean(y * y, axis=-1, keepdims=True)
    y_ref[...] = y * lax.rsqrt(ms + RMS_EPS) * gf_ref[...]


def _out_projection(og, gc, sa, x2, w_attn_bf, w_out_bf, final_g, tm):
    m, d = x2.shape
    row = pl.BlockSpec((tm, d), lambda i: (i, 0))
    wspec = pl.BlockSpec((d, d), lambda i: (0, 0))
    return pl.pallas_call(
        _out_body,
        grid=(m // tm,),
        in_specs=[row, row, row, row, wspec, wspec, pl.BlockSpec((1, d), lambda i: (0, 0))],
        out_specs=row,
        out_shape=jax.ShapeDtypeStruct((m, d), F32),
        compiler_params=pltpu.CompilerParams(dimension_semantics=("parallel",),
                                             vmem_limit_bytes=V7X_VMEM_LIMIT),
        name="out_projection",
    )(og, gc, sa, x2, w_attn_bf, w_out_bf, final_g.reshape(1, d))


def kernel(x_prompt, x_sample, cache_k, cache_v, state_conv, page_table, norm_g, w_in, conv_w, conv_b,
           conv_ln_g, conv_ln_b, w_conv_proj, lambda_q1, lambda_k1, lambda_q2, lambda_k2, subln_g,
           w_attn_proj, w_out, final_norm_g):
    depth = norm_g.shape[0]
    assert depth == 1, "single-layer step"
    b, s, d = x_prompt.shape
    nb, t_new, _ = x_sample.shape
    w_in_bf = w_in[0].astype(BF16)
    w_conv_bf = w_conv_proj[0].astype(BF16)
    w_attn_bf = w_attn_proj[0].astype(BF16)
    w_out_bf = w_out[0].astype(BF16)
    lams = (lambda_q1[0], lambda_k1[0], lambda_q2[0], lambda_k2[0])

    def conv_and_state(u, cg, sc, buf, bsz, seq, bb, ts):
        buf32 = jnp.pad(buf, ((0, 0), (HALO_PAD, 0), (0, 0)))
        gc, nbuf = _conv_branch(u.reshape(bsz, seq, d), cg.reshape(bsz, seq, d), sc.reshape(bsz, seq, d), buf32,
                                conv_w[0], conv_b[0], conv_ln_g[0], conv_ln_b[0], w_conv_bf, bb, ts)
        return gc.reshape(bsz * seq, d), nbuf[:, HALO_PAD:, :]

    xp2 = x_prompt.reshape(b * s, d)
    u, cg, q, k32, kb, v32, vb, ag, sc, sa = _in_projection(xp2, norm_g[0], w_in_bf, 256)
    gc, conv_prompt = conv_and_state(u, cg, sc, jnp.zeros((b, CONV_WIDTH - 1, d), F32), b, s, 1, 256)
    og = _prompt_attention(q.reshape(b, s, d), kb.reshape(b, s, d), vb.reshape(b, s, d), ag.reshape(b, s, d),
                           *lams, subln_g[0], 256, 2)
    y_prompt = _out_projection(og.reshape(b * s, d), gc, sa, xp2, w_attn_bf, w_out_bf, final_norm_g, 512)
    k_prompt = k32.reshape(1, b, s, N_HEADS, V_DIM)
    v_prompt = v32.reshape(1, b, s, N_HEADS, V_DIM)

    xs2 = x_sample.reshape(nb * t_new, d)
    u, cg, q, k32, kb, v32, vb, ag, sc, sa = _in_projection(xs2, norm_g[0], w_in_bf, 256)
    gc, conv_sample = conv_and_state(u, cg, sc, state_conv[0], nb, t_new, 8, t_new)
    k_sample = k32.reshape(nb, t_new, N_HEADS, V_DIM)
    v_sample = v32.reshape(nb, t_new, N_HEADS, V_DIM)
    qh = q.reshape(nb, t_new, N_HEADS, V_DIM).transpose(0, 2, 1, 3)
    wt = jnp.broadcast_to(qh[:, :, None], (nb, N_HEADS, 2, t_new, V_DIM)).reshape(nb, N_HEADS * 2 * t_new, V_DIM)
    n_pool, page = cache_k.shape[1], cache_k.shape[2]
    rows_kv = (page * N_HEADS, V_DIM)
    on = _paged_attention(page_table, wt, cache_k.reshape(n_pool, *rows_kv), cache_v.reshape(n_pool, *rows_kv),
                          k32.reshape(nb, t_new * N_HEADS, V_DIM), v32.reshape(nb, t_new * N_HEADS, V_DIM),
                          8)
    o5 = on.reshape(nb, N_HEADS, 2, t_new, V_DIM).transpose(2, 0, 3, 1, 4)
    o1 = o5[0].reshape(nb * t_new, d)
    o2 = o5[1].reshape(nb * t_new, d)
    og = _sample_post(o1, o2, ag, *lams, subln_g[0])
    y_sample = _out_projection(og, gc, sa, xs2, w_attn_bf, w_out_bf, final_norm_g, 256)

    return (y_prompt.reshape(b, s, d), y_sample.reshape(nb, t_new, d), k_prompt, v_prompt, conv_prompt[None],
            k_sample[None], v_sample[None], conv_sample[None])
```

```python
import math

import jax
import jax.numpy as jnp
from jax import lax
from jax.experimental import pallas as pl
from jax.experimental.pallas import tpu as pltpu

F32 = jnp.float32
BF16 = jnp.bfloat16

N_HEADS = 8
HEAD_DIM = 64
V_DIM = 2 * HEAD_DIM
CONV_WIDTH = 31
CONV_HALO = 32
HALO_PAD = CONV_HALO - (CONV_WIDTH - 1)
SUBLANES = 8
CONV_ROW_CHUNK = 64
CONV_COL_CHUNK = 256
RMS_EPS = 1e-6
SUBLN_EPS = 1e-5
LN_EPS = 1e-5
NEG_INIT = -1e30
LAMBDA_INIT = 0.8 - 0.6 * math.exp(-0.3 * 0)
Q_SCALE = HEAD_DIM ** -0.5 * math.log2(math.e)
V7X_VMEM_LIMIT = 56 * 1024 * 1024

IN_PROJ_ROWS = 256
CONV_ROWS = 256
SAMPLE_CONV_BATCH = 8
ATTN_Q_ROWS = 256
ATTN_HEADS_PER_STEP = 4
OUT_PROJ_ROWS = 512
PAGES_PER_STEP = 8

_TN = (((0,), (0,)), ((), ()))


def _sigmoid(z):
    return 1.0 / (1.0 + jnp.exp(-z))


def _silu(z):
    return z * _sigmoid(z)


def _lambda_value(lq1, lk1, lq2, lk2):
    a = jnp.sum(lq1 * lk1, axis=-1, keepdims=True)
    b = jnp.sum(lq2 * lk2, axis=-1, keepdims=True)
    return jnp.exp(a) - jnp.exp(b) + LAMBDA_INIT


def _inproj_body(x_ref, g_ref, w_ref, u_ref, cg_ref, q_ref, k32_ref, kb_ref, v32_ref, vb_ref,
                 ag_ref, sc_ref, sa_ref):
    d = x_ref.shape[-1]
    x = x_ref[...]
    ms = jnp.mean(x * x, axis=-1, keepdims=True)
    h = (x * lax.rsqrt(ms + RMS_EPS) * g_ref[...]).astype(BF16)

    def proj(i):
        return jnp.dot(h, w_ref[:, i * d:(i + 1) * d], preferred_element_type=F32)

    u_ref[...] = (proj(0) * _sigmoid(proj(1))).astype(BF16)
    cg_ref[...] = _silu(proj(2)).astype(BF16)
    q_ref[...] = (proj(3) * Q_SCALE).astype(BF16)
    k = proj(4)
    k32_ref[...] = k
    kb_ref[...] = k.astype(BF16)
    v = proj(5)
    v32_ref[...] = v
    vb_ref[...] = v.astype(BF16)
    ag_ref[...] = _silu(proj(6)).astype(BF16)
    sc_ref[...] = _sigmoid(proj(7)).astype(BF16)
    sa_ref[...] = _sigmoid(proj(8)).astype(BF16)


def _in_projection(x2, norm_g, w_in_bf, tm):
    m, d = x2.shape
    d_in = w_in_bf.shape[1]
    row = pl.BlockSpec((tm, d), lambda i: (i, 0))
    out_dtypes = [BF16, BF16, BF16, F32, BF16, F32, BF16, BF16, BF16, BF16]
    return pl.pallas_call(
        _inproj_body,
        grid=(m // tm,),
        in_specs=[row,
                  pl.BlockSpec((1, d), lambda i: (0, 0)),
                  pl.BlockSpec((d, d_in), lambda i: (0, 0), pipeline_mode=pl.Buffered(1))],
        out_specs=[row] * len(out_dtypes),
        out_shape=[jax.ShapeDtypeStruct((m, d), t) for t in out_dtypes],
        compiler_params=pltpu.CompilerParams(dimension_semantics=("parallel",),
                                             vmem_limit_bytes=V7X_VMEM_LIMIT),
        name="in_projection",
    )(x2, norm_g.reshape(1, d), w_in_bf)


def _conv_body(u_ref, cg_ref, sc_ref, buf_ref, cw_ref, cb_ref, lg_ref, lb_ref, w_ref,
               gc_ref, nbuf_ref, full_ref, sh_ref, conv_ref):
    bb, ts, d = u_ref.shape
    s = pl.program_id(1)

    @pl.when(s == 0)
    def _():
        full_ref[:, 0:CONV_HALO, :] = buf_ref[...]

    full_ref[:, CONV_HALO:CONV_HALO + ts, :] = u_ref[...].astype(F32)
    n_sh = sh_ref.shape[2]
    for r in range(1, SUBLANES):
        sh_ref[r - 1] = full_ref[:, r:r + n_sh, :]

    rc = min(CONV_ROW_CHUNK, ts)

    def chunk(ci, carry):
        base = pl.multiple_of(ci * rc, rc)
        for bi in range(bb):
            for c0 in range(0, d, CONV_COL_CHUNK):
                cols = slice(c0, c0 + CONV_COL_CHUNK)
                acc = jnp.broadcast_to(cb_ref[:, cols], (rc, CONV_COL_CHUNK))
                for j in range(CONV_WIDTH):
                    r = (HALO_PAD + j) % SUBLANES
                    a = HALO_PAD + j - r
                    src = full_ref if r == 0 else sh_ref.at[r - 1]
                    acc = acc + cw_ref[j:j + 1, cols] * src[bi, pl.ds(base + a, rc), cols]
                conv_ref[bi, pl.ds(base, rc), cols] = acc
        return carry

    lax.fori_loop(0, ts // rc, chunk, 0)
    acc = conv_ref[...]
    mu = jnp.mean(acc, axis=-1, keepdims=True)
    cen = acc - mu
    var = jnp.mean(cen * cen, axis=-1, keepdims=True)
    y = cen * lax.rsqrt(var + LN_EPS) * lg_ref[...].reshape(1, 1, d) + lb_ref[...].reshape(1, 1, d)
    c = _silu(y) * cg_ref[...].astype(F32)
    yc = jnp.dot(c.astype(BF16).reshape(bb * ts, d), w_ref[...], preferred_element_type=F32)
    gc_ref[...] = (sc_ref[...].astype(F32) * yc.reshape(bb, ts, d)).astype(BF16)
    tail = full_ref[:, ts:ts + CONV_HALO, :]
    full_ref[:, 0:CONV_HALO, :] = tail
    nbuf_ref[...] = tail


def _conv_branch(u3, cg3, sc3, buf32, conv_w, conv_b, ln_g, ln_b, w_proj_bf, bb, ts):
    b, s, d = u3.shape
    tile = pl.BlockSpec((bb, ts, d), lambda i, j: (i, j, 0))
    halo = pl.BlockSpec((bb, CONV_HALO, d), lambda i, j: (i, 0, 0))
    vec = pl.BlockSpec((1, d), lambda i, j: (0, 0))
    return pl.pallas_call(
        _conv_body,
        grid=(b // bb, s // ts),
        in_specs=[tile, tile, tile, halo,
                  pl.BlockSpec((CONV_WIDTH, d), lambda i, j: (0, 0)),
                  vec, vec, vec,
                  pl.BlockSpec((d, d), lambda i, j: (0, 0))],
        out_specs=[tile, halo],
        out_shape=[jax.ShapeDtypeStruct((b, s, d), BF16),
                   jax.ShapeDtypeStruct((b, CONV_HALO, d), F32)],
        scratch_shapes=[pltpu.VMEM((bb, CONV_HALO + ts, d), F32),
                        pltpu.VMEM((SUBLANES - 1, bb, CONV_HALO - SUBLANES + ts, d), F32),
                        pltpu.VMEM((bb, ts, d), F32)],
        compiler_params=pltpu.CompilerParams(dimension_semantics=("parallel", "arbitrary"),
                                             vmem_limit_bytes=V7X_VMEM_LIMIT),
        name="conv_branch",
    )(u3, cg3, sc3, buf32, conv_w, conv_b.reshape(1, d), ln_g.reshape(1, d), ln_b.reshape(1, d), w_proj_bf)


def _head_post(o1, o2, lam, sg, ag):
    dlt = o1 - lam * o2
    ms = jnp.mean(dlt * dlt, axis=-1, keepdims=True)
    r = dlt * lax.rsqrt(ms + SUBLN_EPS) * sg * (1.0 - LAMBDA_INIT)
    return (r * ag.astype(F32)).astype(BF16)


def _prompt_attn_body(lq1_ref, lk1_ref, lq2_ref, lk2_ref, sg_ref, q_ref, k_ref, v_ref, ag_ref,
                      o_ref, vt_sc, acc_sc):
    tq = q_ref.shape[0]
    e = V_DIM
    hps = q_ref.shape[1] // e
    heads = range(hps)
    qi = pl.program_id(2)

    @pl.when(qi == 0)
    def _():
        for h in heads:
            vt_sc[h] = v_ref[:, h * e:(h + 1) * e].T

    feat = lax.broadcasted_iota(jnp.int32, (e, tq), 0)
    qst = []
    for h in heads:
        qt = q_ref[:, h * e:(h + 1) * e].T
        zero = jnp.zeros_like(qt)
        qst.append(jnp.concatenate([jnp.where(feat < HEAD_DIM, qt, zero), jnp.where(feat >= HEAD_DIM, qt, zero)],
                                   axis=1))
    acc_sc[...] = jnp.zeros(acc_sc.shape, F32)

    def step(block, nk, carry, masked):
        start = pl.multiple_of(block * tq, tq)
        out = []
        for h in heads:
            m_prev, l_prev = carry[h]
            kb = k_ref[pl.ds(start, nk), h * e:(h + 1) * e]
            vt = vt_sc[h, :, pl.ds(start, nk)]
            st = jnp.dot(kb, qst[h], preferred_element_type=F32)
            if masked:
                key = start + lax.broadcasted_iota(jnp.int32, st.shape, 0)
                col = lax.broadcasted_iota(jnp.int32, st.shape, 1)
                qpos = qi * tq + jnp.where(col >= tq, col - tq, col)
                st = jnp.where(key <= qpos, st, NEG_INIT)
            m_new = jnp.maximum(m_prev, jnp.max(st, axis=0, keepdims=True))
            alpha = jnp.exp2(m_prev - m_new)
            p = jnp.exp2(st - m_new)
            l_new = l_prev * alpha + jnp.sum(p, axis=0, keepdims=True)
            acc_sc[h] = acc_sc[h] * alpha + jnp.dot(vt, p.astype(BF16), preferred_element_type=F32)
            out.append((m_new, l_new))
        return tuple(out)

    init = tuple((jnp.full((1, 2 * tq), NEG_INIT, F32), jnp.zeros((1, 2 * tq), F32)) for _ in heads)
    carry = lax.fori_loop(0, qi // 2, lambda jj, c: step(2 * jj, 2 * tq, c, False), init)
    carry = lax.cond(qi % 2 == 1,
                     lambda c: step(qi - 1, 2 * tq, c, True),
                     lambda c: step(qi, tq, c, True), carry)

    lam = _lambda_value(lq1_ref[...], lk1_ref[...], lq2_ref[...], lk2_ref[...])
    for h in heads:
        ot = acc_sc[h] / carry[h][1]
        dlt = ot[:, :tq] - lam * ot[:, tq:]
        ms = jnp.mean(dlt * dlt, axis=0, keepdims=True)
        r = (dlt * lax.rsqrt(ms + SUBLN_EPS)).T
        cols = slice(h * e, (h + 1) * e)
        o_ref[:, cols] = (r * sg_ref[...] * (1.0 - LAMBDA_INIT) * ag_ref[:, cols].astype(F32)).astype(BF16)


def _prompt_attention(q3, k3, v3, ag3, lq1, lk1, lq2, lk2, subln_g, tq, hps):
    b, s, d = q3.shape
    lam_spec = pl.BlockSpec((1, HEAD_DIM), lambda i, h, j: (0, 0))
    qtile = pl.BlockSpec((None, tq, hps * V_DIM), lambda i, h, j: (i, j, h))
    kv = pl.BlockSpec((None, s, hps * V_DIM), lambda i, h, j: (i, 0, h))
    return pl.pallas_call(
        _prompt_attn_body,
        grid=(b, N_HEADS // hps, s // tq),
        in_specs=[lam_spec, lam_spec, lam_spec, lam_spec,
                  pl.BlockSpec((1, V_DIM), lambda i, h, j: (0, 0)),
                  qtile, kv, kv, qtile],
        out_specs=qtile,
        out_shape=jax.ShapeDtypeStruct((b, s, d), BF16),
        scratch_shapes=[pltpu.VMEM((hps, V_DIM, s), BF16), pltpu.VMEM((hps, V_DIM, 2 * tq), F32)],
        compiler_params=pltpu.CompilerParams(dimension_semantics=("parallel", "parallel", "arbitrary"),
                                             vmem_limit_bytes=V7X_VMEM_LIMIT),
        name="prompt_attention",
    )(lq1.reshape(1, -1), lk1.reshape(1, -1), lq2.reshape(1, -1), lk2.reshape(1, -1),
      subln_g.reshape(1, -1), q3, k3, v3, ag3)


PAGE_RING_DEPTH = 3


def _paged_attn_body(pt_ref, wt_ref, knew_ref, vnew_ref, ck_hbm, cv_hbm,
                     o_ref, m_sc, l_sc, acc_sc, w_sc, kc_sc, vc_sc, kbuf, vbuf, sem):
    nh = N_HEADS
    _, pps, rows, e = kbuf.shape
    page = rows // nh
    ncol = wt_ref.shape[0]
    t_new = knew_ref.shape[0] // nh
    cols_per_head = ncol // nh
    c = pl.program_id(1)
    n_c = pl.num_programs(1)
    g = pl.program_id(0) * n_c + c
    n_steps = pl.num_programs(0) * n_c

    def page_copies(step):
        slot = lax.rem(step, PAGE_RING_DEPTH)
        sb = lax.div(step, n_c)
        sc = lax.rem(step, n_c)
        copies = []
        for i in range(pps):
            pg = pt_ref[sb, sc * pps + i]
            copies.append(pltpu.make_async_copy(ck_hbm.at[pg], kbuf.at[slot, i], sem.at[slot, 0, i]))
            copies.append(pltpu.make_async_copy(cv_hbm.at[pg], vbuf.at[slot, i], sem.at[slot, 1, i]))
        return copies

    def start_pages(step):
        for cp in page_copies(step):
            cp.start()

    @pl.when(g == 0)
    def _():
        for s in range(PAGE_RING_DEPTH - 1):
            start_pages(jnp.int32(s))

    @pl.when(g + (PAGE_RING_DEPTH - 1) < n_steps)
    def _():
        start_pages(g + (PAGE_RING_DEPTH - 1))

    for cp in page_copies(g):
        cp.wait()
    slot = lax.rem(g, PAGE_RING_DEPTH)

    @pl.when(c == 0)
    def _():
        m_sc[...] = jnp.full(m_sc.shape, NEG_INIT, F32)
        l_sc[...] = jnp.zeros(l_sc.shape, F32)
        acc_sc[...] = jnp.zeros(acc_sc.shape, F32)
        wt = wt_ref[...]
        r = lax.broadcasted_iota(jnp.int32, (ncol, e), 0)
        f = lax.broadcasted_iota(jnp.int32, (ncol, e), 1)
        same_map = ((r // t_new) % 2) == (f // HEAD_DIM)
        wmt = jnp.where(same_map, wt, jnp.zeros_like(wt)).T
        col_head = lax.broadcasted_iota(jnp.int32, (e, ncol), 1) // cols_per_head
        for h in range(nh):
            w_sc[h * e:(h + 1) * e, :] = jnp.where(col_head == h, wmt, jnp.zeros_like(wmt))

    def regroup(src_ref, n):
        return [src_ref[pl.ds(h, n, stride=nh), :] for h in range(nh)]

    for i in range(pps):
        for h, (kh, vh) in enumerate(zip(regroup(kbuf.at[slot, i], page), regroup(vbuf.at[slot, i], page))):
            kc_sc[i * page:(i + 1) * page, h * e:(h + 1) * e] = kh.astype(BF16)
            vc_sc[i * page:(i + 1) * page, h * e:(h + 1) * e] = vh.astype(BF16)

    def as_column(row_vec):
        return jnp.broadcast_to(row_vec, (e, ncol)).T

    def update(carry, kc, vc, causal, n_split):
        m_prev, l_prev, acc = carry
        n = kc.shape[0]
        rb = n // n_split
        st = jnp.concatenate([jnp.dot(kc[i * rb:(i + 1) * rb], w_sc[...], preferred_element_type=F32)
                              for i in range(n_split)], axis=0)
        if causal:
            key_pos = lax.broadcasted_iota(jnp.int32, st.shape, 0)
            q_pos = lax.broadcasted_iota(jnp.int32, st.shape, 1) % t_new
            st = jnp.where(key_pos <= q_pos, st, NEG_INIT)
        m_new = jnp.maximum(m_prev, jnp.max(st, axis=0, keepdims=True))
        alpha = jnp.exp2(m_prev - m_new)
        p = jnp.exp2(st - m_new)
        l_new = l_prev * alpha + jnp.sum(p, axis=0, keepdims=True)
        pb = p.astype(BF16)
        hg = nh // n_split
        blocks = []
        for g in range(n_split):
            pv_g = lax.dot_general(pb, vc[:, g * hg * e:(g + 1) * hg * e], _TN,
                                   preferred_element_type=F32)
            for j in range(hg):
                h = g * hg + j
                blocks.append(pv_g[h * cols_per_head:(h + 1) * cols_per_head, j * e:(j + 1) * e])
        pv = jnp.concatenate(blocks, axis=0)
        return m_new, l_new, acc * as_column(alpha) + pv

    carry = update((m_sc[...], l_sc[...], acc_sc[...]), kc_sc[...], vc_sc[...], False, 2)
    m_sc[...], l_sc[...], acc_sc[...] = carry

    @pl.when(c == pl.num_programs(1) - 1)
    def _():
        pad = jnp.zeros((t_new, nh * e), F32)
        kn = jnp.concatenate([jnp.concatenate(regroup(knew_ref, t_new), axis=1), pad], axis=0).astype(BF16)
        vn = jnp.concatenate([jnp.concatenate(regroup(vnew_ref, t_new), axis=1), pad], axis=0).astype(BF16)
        _, l_fin, acc = update(carry, kn, vn, True, 1)
        o_ref[...] = acc / as_column(l_fin)


def _paged_attention(page_table, wt, cache_k, cache_v, k_new, v_new, pps):
    nb, n_pages = page_table.shape
    _, rows, e = cache_k.shape
    ncol = wt.shape[1]
    new_rows = k_new.shape[1]

    assert nb * (n_pages // pps) >= PAGE_RING_DEPTH - 1
    new_spec = pl.BlockSpec((None, new_rows, e), lambda b, c, pt: (b, 0, 0))
    hbm = pl.BlockSpec(memory_space=pl.ANY)
    grid_spec = pltpu.PrefetchScalarGridSpec(
        num_scalar_prefetch=1,
        grid=(nb, n_pages // pps),
        in_specs=[pl.BlockSpec((None, ncol, e), lambda b, c, pt: (b, 0, 0)), new_spec, new_spec, hbm, hbm],
        out_specs=pl.BlockSpec((None, ncol, e), lambda b, c, pt: (b, 0, 0)),
        scratch_shapes=[pltpu.VMEM((1, ncol), F32), pltpu.VMEM((1, ncol), F32), pltpu.VMEM((ncol, e), F32),
                        pltpu.VMEM((N_HEADS * e, ncol), BF16),
                        pltpu.VMEM((pps * rows // N_HEADS, N_HEADS * e), BF16),
                        pltpu.VMEM((pps * rows // N_HEADS, N_HEADS * e), BF16),
                        pltpu.VMEM((PAGE_RING_DEPTH, pps, rows, e), F32),
                        pltpu.VMEM((PAGE_RING_DEPTH, pps, rows, e), F32),
                        pltpu.SemaphoreType.DMA((PAGE_RING_DEPTH, 2, pps))],
    )
    return pl.pallas_call(
        _paged_attn_body,
        grid_spec=grid_spec,
        out_shape=jax.ShapeDtypeStruct((nb, ncol, e), F32),
        compiler_params=pltpu.CompilerParams(dimension_semantics=("arbitrary", "arbitrary"),
                                             vmem_limit_bytes=V7X_VMEM_LIMIT),
        name="paged_attention",
    )(page_table, wt, k_new, v_new, cache_k, cache_v)


def _sample_post_body(lq1_ref, lk1_ref, lq2_ref, lk2_ref, sg_ref, o1_ref, o2_ref, ag_ref, og_ref):
    lam = _lambda_value(lq1_ref[...], lk1_ref[...], lq2_ref[...], lk2_ref[...])
    for h in range(N_HEADS):
        cols = slice(h * V_DIM, (h + 1) * V_DIM)
        og_ref[:, cols] = _head_post(o1_ref[:, cols], o2_ref[:, cols], lam, sg_ref[...], ag_ref[:, cols])


def _sample_post(o1, o2, ag, lq1, lk1, lq2, lk2, subln_g):
    m, d = o1.shape
    full = pl.BlockSpec((m, d), lambda i: (0, 0))
    lam_spec = pl.BlockSpec((1, HEAD_DIM), lambda i: (0, 0))
    return pl.pallas_call(
        _sample_post_body,
        grid=(1,),
        in_specs=[lam_spec, lam_spec, lam_spec, lam_spec, pl.BlockSpec((1, V_DIM), lambda i: (0, 0)),
                  full, full, full],
        out_specs=full,
        out_shape=jax.ShapeDtypeStruct((m, d), BF16),
        compiler_params=pltpu.CompilerParams(vmem_limit_bytes=V7X_VMEM_LIMIT),
        name="sample_attention_post",
    )(lq1.reshape(1, -1), lk1.reshape(1, -1), lq2.reshape(1, -1), lk2.reshape(1, -1),
      subln_g.reshape(1, -1), o1, o2, ag)


def _out_body(og_ref, gc_ref, sa_ref, x_ref, wap_ref, wo_ref, gf_ref, y_ref):
    ya = jnp.dot(og_ref[...], wap_ref[...], preferred_element_type=F32)
    merged = gc_ref[...].astype(F32) + sa_ref[...].astype(F32) * ya
    y = x_ref[...] + jnp.dot(merged.astype(BF16), wo_ref[...], preferred_element_type=F32)
    ms = jnp.mean(y * y, axis=-1, keepdims=True)
    y_ref[...] = y * lax.rsqrt(ms + RMS_EPS) * gf_ref[...]


def _out_projection(og, gc, sa, x2, w_attn_bf, w_out_bf, final_g, tm):
    m, d = x2.shape
    row = pl.BlockSpec((tm, d), lambda i: (i, 0))
    wspec = pl.BlockSpec((d, d), lambda i: (0, 0))
    return pl.pallas_call(
        _out_body,
        grid=(m // tm,),
        in_specs=[row, row, row, row, wspec, wspec, pl.BlockSpec((1, d), lambda i: (0, 0))],
        out_specs=row,
        out_shape=jax.ShapeDtypeStruct((m, d), F32),
        compiler_params=pltpu.CompilerParams(dimension_semantics=("parallel",),
                                             vmem_limit_bytes=V7X_VMEM_LIMIT),
        name="out_projection",
    )(og, gc, sa, x2, w_attn_bf, w_out_bf, final_g.reshape(1, d))


def kernel(x_prompt, x_sample, cache_k, cache_v, state_conv, page_table, norm_g, w_in, conv_w, conv_b,
           conv_ln_g, conv_ln_b, w_conv_proj, lambda_q1, lambda_k1, lambda_q2, lambda_k2, subln_g,
           w_attn_proj, w_out, final_norm_g):
    depth = norm_g.shape[0]
    assert depth == 1, "single-layer step"
    b, s, d = x_prompt.shape
    nb, t_new, _ = x_sample.shape
    w_in_bf = w_in[0].astype(BF16)
    w_conv_bf = w_conv_proj[0].astype(BF16)
    w_attn_bf = w_attn_proj[0].astype(BF16)
    w_out_bf = w_out[0].astype(BF16)
    lams = (lambda_q1[0], lambda_k1[0], lambda_q2[0], lambda_k2[0])

    def conv_and_state(u, cg, sc, buf, bsz, seq, bb, ts):
        buf32 = jnp.pad(buf, ((0, 0), (HALO_PAD, 0), (0, 0)))
        gc, nbuf = _conv_branch(u.reshape(bsz, seq, d), cg.reshape(bsz, seq, d), sc.reshape(bsz, seq, d), buf32,
                                conv_w[0], conv_b[0], conv_ln_g[0], conv_ln_b[0], w_conv_bf, bb, ts)
        return gc.reshape(bsz * seq, d), nbuf[:, HALO_PAD:, :]

    xp2 = x_prompt.reshape(b * s, d)
    u, cg, q, k32, kb, v32, vb, ag, sc, sa = _in_projection(xp2, norm_g[0], w_in_bf, IN_PROJ_ROWS)
    gc, conv_prompt = conv_and_state(u, cg, sc, jnp.zeros((b, CONV_WIDTH - 1, d), F32), b, s, 1, CONV_ROWS)
    og = _prompt_attention(q.reshape(b, s, d), kb.reshape(b, s, d), vb.reshape(b, s, d), ag.reshape(b, s, d),
                           *lams, subln_g[0], ATTN_Q_ROWS, ATTN_HEADS_PER_STEP)
    y_prompt = _out_projection(og.reshape(b * s, d), gc, sa, xp2, w_attn_bf, w_out_bf, final_norm_g, OUT_PROJ_ROWS)
    k_prompt = k32.reshape(1, b, s, N_HEADS, V_DIM)
    v_prompt = v32.reshape(1, b, s, N_HEADS, V_DIM)

    xs2 = x_sample.reshape(nb * t_new, d)
    u, cg, q, k32, kb, v32, vb, ag, sc, sa = _in_projection(xs2, norm_g[0], w_in_bf, IN_PROJ_ROWS)
    gc, conv_sample = conv_and_state(u, cg, sc, state_conv[0], nb, t_new, SAMPLE_CONV_BATCH, t_new)
    k_sample = k32.reshape(nb, t_new, N_HEADS, V_DIM)
    v_sample = v32.reshape(nb, t_new, N_HEADS, V_DIM)
    qh = q.reshape(nb, t_new, N_HEADS, V_DIM).transpose(0, 2, 1, 3)
    wt = jnp.broadcast_to(qh[:, :, None], (nb, N_HEADS, 2, t_new, V_DIM)).reshape(nb, N_HEADS * 2 * t_new, V_DIM)
    n_pool, page = cache_k.shape[1], cache_k.shape[2]
    rows_kv = (page * N_HEADS, V_DIM)
    on = _paged_attention(page_table, wt, cache_k.reshape(n_pool, *rows_kv), cache_v.reshape(n_pool, *rows_kv),
                          k32.reshape(nb, t_new * N_HEADS, V_DIM), v32.reshape(nb, t_new * N_HEADS, V_DIM),
                          PAGES_PER_STEP)
    o5 = on.reshape(nb, N_HEADS, 2, t_new, V_DIM).transpose(2, 0, 3, 1, 4)
    o1 = o5[0].reshape(nb * t_new, d)
    o2 = o5[1].reshape(nb * t_new, d)
    og = _sample_post(o1, o2, ag, *lams, subln_g[0])
    y_sample = _out_projection(og, gc, sa, xs2, w_attn_bf, w_out_bf, final_norm_g, IN_PROJ_ROWS)

    return (y_prompt.reshape(b, s, d), y_sample.reshape(nb, t_new, d), k_prompt, v_prompt, conv_prompt[None],
            k_sample[None], v_sample[None], conv_sample[None])
```
